```python
import jax, jax.numpy as jnp
from jax import lax
import numpy as np

D_MODEL = 2048
BATCH = 4
SEQ = 4096
DEPTH = 1

D_MIX = D_MODEL
RET_HEAD_DIM = 128
RET_HEADS = (D_MIX // 2) // RET_HEAD_DIM
RET_WIDTH = RET_HEADS * RET_HEAD_DIM
RET_CHUNK = 128
RET_THETA = 10000.0
RET_DECAY_BASE = 5
GN_EPS = 1e-5
ATT_HEAD_DIM = 128
ATT_HEADS = (D_MIX - RET_WIDTH) // ATT_HEAD_DIM
ATT_KV_HEADS = 2
ATT_WIDTH = ATT_HEADS * ATT_HEAD_DIM
ATT_KV_WIDTH = ATT_KV_HEADS * ATT_HEAD_DIM
WINDOW = 128
ATT_BLOCK = 128
ROPE_THETA = 500000.0
ROPE_DIMS = ATT_HEAD_DIM // 4
NORM_EPS = 1e-6
MASK_VALUE = -1e30
D_IN_PROJ = 4 * RET_WIDTH + 2 * ATT_WIDTH + 2 * ATT_KV_WIDTH

kernel_name = "hybrid_retention_window_gqa_block"


def rms_norm(x, w):
    xf = x.astype(jnp.float32)
    y = xf * lax.rsqrt(jnp.mean(xf * xf, axis=-1, keepdims=True) + NORM_EPS)
    return (y * w.astype(jnp.float32)).astype(x.dtype)


def rotary(x, rot_dims, theta):
    S = x.shape[1]
    half = rot_dims // 2
    inv_freq = theta ** (-jnp.arange(half, dtype=jnp.float32) / half)
    ang = jnp.arange(S, dtype=jnp.float32)[:, None] * inv_freq[None, :]
    cos = jnp.cos(ang)[None, :, None, :]
    sin = jnp.sin(ang)[None, :, None, :]
    xr = x[..., :rot_dims].astype(jnp.float32)
    x1, x2 = xr[..., :half], xr[..., half:]
    rot = jnp.concatenate([x1 * cos - x2 * sin, x2 * cos + x1 * sin], axis=-1).astype(x.dtype)
    return jnp.concatenate([rot, x[..., rot_dims:]], axis=-1)


def retention_chunkwise(q, k, v, log_gamma, include_diag):
    B, H, S, Dk = q.shape
    Dv = v.shape[-1]
    C = RET_CHUNK
    NC = S // C
    qc = q.reshape(B, H, NC, C, Dk)
    kc = k.reshape(B, H, NC, C, Dk)
    vc = v.reshape(B, H, NC, C, Dv)
    idx = jnp.arange(C, dtype=jnp.float32)
    diff = idx[:, None] - idx[None, :]
    mask = (diff >= 0) if include_diag else (diff > 0)
    decay = jnp.where(mask[None], jnp.exp(log_gamma[:, None, None] * jnp.where(mask, diff, 0.0)[None]), 0.0)
    scores = jnp.einsum('bhnik,bhnjk->bhnij', qc, kc) * decay[None, :, None]
    inner = jnp.einsum('bhnij,bhnjv->bhniv', scores, vc)
    k_dec = jnp.exp(log_gamma[:, None] * (C - 1 - idx)[None, :])
    kv_chunk = jnp.einsum('bhnjk,hj,bhnjv->nbhkv', kc, k_dec, vc)
    chunk_decay = jnp.exp(log_gamma * C)[None, :, None, None]

    def step(state, kv):
        return state * chunk_decay + kv, state

    _, prev = lax.scan(step, jnp.zeros((B, H, Dk, Dv), jnp.float32), kv_chunk)
    q_dec = jnp.exp(log_gamma[:, None] * (idx + 1.0)[None, :])
    cross = jnp.einsum('bhnik,hi,nbhkv->bhniv', qc, q_dec, prev)
    return (inner + cross).reshape(B, H, S, Dv)


def retention_branch(rq, rk, rv, decay_fwd, decay_bwd, gn_w, gn_b):
    B, S, _ = rq.shape
    q = rotary(rq.reshape(B, S, RET_HEADS, RET_HEAD_DIM), RET_HEAD_DIM, RET_THETA)
    k = rotary(rk.reshape(B, S, RET_HEADS, RET_HEAD_DIM), RET_HEAD_DIM, RET_THETA)
    q = q.astype(jnp.float32).transpose(0, 2, 1, 3)
    k = k.astype(jnp.float32).transpose(0, 2, 1, 3) * (RET_HEAD_DIM ** -0.5)
    v = rv.reshape(B, S, RET_HEADS, RET_HEAD_DIM).astype(jnp.float32).transpose(0, 2, 1, 3)
    lg_f = jax.nn.log_sigmoid(decay_fwd.astype(jnp.float32))
    lg_b = jax.nn.log_sigmoid(decay_bwd.astype(jnp.float32))
    y_f = retention_chunkwise(q, k, v, lg_f, True)
    y_b = retention_chunkwise(q[:, :, ::-1], k[:, :, ::-1], v[:, :, ::-1], lg_b, False)[:, :, ::-1]
    y = (y_f + y_b).transpose(0, 2, 1, 3)
    mu = jnp.mean(y, axis=-1, keepdims=True)
    var = jnp.mean(jnp.square(y - mu), axis=-1, keepdims=True)
    y = ((y - mu) * lax.rsqrt(var + GN_EPS)).reshape(B, S, RET_WIDTH)
    return y * gn_w.astype(jnp.float32) + gn_b.astype(jnp.float32)


def window_attention_branch(aq, ak, av, sink):
    B, S, _ = aq.shape
    T = ATT_BLOCK
    NB = S // T
    G = ATT_HEADS // ATT_KV_HEADS
    q = rotary(aq.reshape(B, S, ATT_HEADS, ATT_HEAD_DIM), ROPE_DIMS, ROPE_THETA)
    k = rotary(ak.reshape(B, S, ATT_KV_HEADS, ATT_HEAD_DIM), ROPE_DIMS, ROPE_THETA)
    v = av.reshape(B, S, ATT_KV_HEADS, ATT_HEAD_DIM)
    qb = q.reshape(B, NB, T, ATT_KV_HEADS, G, ATT_HEAD_DIM)
    pad = ((0, 0), (T, T), (0, 0), (0, 0))
    kp = jnp.pad(k, pad).reshape(B, NB + 2, T, ATT_KV_HEADS, ATT_HEAD_DIM)
    vp = jnp.pad(v, pad).reshape(B, NB + 2, T, ATT_KV_HEADS, ATT_HEAD_DIM)
    kw = jnp.concatenate([kp[:, :-2], kp[:, 1:-1], kp[:, 2:]], axis=2)
    vw = jnp.concatenate([vp[:, :-2], vp[:, 1:-1], vp[:, 2:]], axis=2)
    qpos = jnp.arange(S).reshape(NB, T)
    kpos_p = (jnp.arange(S + 2 * T) - T).reshape(NB + 2, T)
    kpos = jnp.concatenate([kpos_p[:-2], kpos_p[1:-1], kpos_p[2:]], axis=1)
    valid = (jnp.abs(qpos[:, :, None] - kpos[:, None, :]) <= WINDOW) & (kpos[:, None, :] >= 0) & (kpos[:, None, :] < S)
    s = jnp.einsum('bntkgd,bnskd->bnkgts', qb.astype(jnp.float32), kw.astype(jnp.float32)) * (ATT_HEAD_DIM ** -0.5)
    s = jnp.where(valid[None, :, None, None], s, MASK_VALUE)
    sink_l = sink.astype(jnp.float32).reshape(ATT_KV_HEADS, G)[None, None, :, :, None, None]
    m = jnp.maximum(jnp.max(s, axis=-1, keepdims=True), sink_l)
    p = jnp.exp(s - m)
    denom = jnp.sum(p, axis=-1, keepdims=True) + jnp.exp(sink_l - m)
    out = jnp.einsum('bnkgts,bnskd->bntkgd', p / denom, vw.astype(jnp.float32))
    return out.reshape(B, S, ATT_WIDTH)


def hybrid_layer(x, w_in, w_out, norm_w, decay_fwd, decay_bwd, gn_w, gn_b, sink):
    h = rms_norm(x, norm_w)
    proj = h @ w_in
    splits = [RET_WIDTH, 2 * RET_WIDTH, 3 * RET_WIDTH, 4 * RET_WIDTH,
              4 * RET_WIDTH + ATT_WIDTH, 4 * RET_WIDTH + ATT_WIDTH + ATT_KV_WIDTH,
              4 * RET_WIDTH + ATT_WIDTH + 2 * ATT_KV_WIDTH]
    rq, rk, rv, rg, aq, ak, av, ag = jnp.split(proj, splits, axis=-1)
    y_ret = retention_branch(rq, rk, rv, decay_fwd, decay_bwd, gn_w, gn_b) * jax.nn.silu(rg.astype(jnp.float32))
    y_att = window_attention_branch(aq, ak, av, sink) * jax.nn.silu(ag.astype(jnp.float32))
    y = jnp.concatenate([y_ret, y_att], axis=-1).astype(x.dtype)
    return x + y @ w_out


def setup_inputs(seed: int = 0) -> dict:
    key = jax.random.key(seed)
    ks = jax.random.split(key, 10)
    x = jax.random.normal(ks[0], (BATCH, SEQ, D_MODEL), jnp.float32)
    w_in = jax.random.normal(ks[1], (DEPTH, D_MODEL, D_IN_PROJ), jnp.float32) * (D_MODEL ** -0.5)
    w_out = jax.random.normal(ks[2], (DEPTH, D_MIX, D_MODEL), jnp.float32) * (D_MIX ** -0.5)
    norm_w = 1.0 + 0.01 * jax.random.normal(ks[3], (DEPTH, D_MODEL), jnp.float32)
    exps = RET_DECAY_BASE + jnp.arange(RET_HEADS, dtype=jnp.float32)
    base_logit = jnp.log(2.0 ** exps - 1.0)
    decay_fwd = base_logit[None] + 0.01 * jax.random.normal(ks[4], (DEPTH, RET_HEADS), jnp.float32)
    decay_bwd = base_logit[None] + 0.01 * jax.random.normal(ks[5], (DEPTH, RET_HEADS), jnp.float32)
    gn_w = 1.0 + 0.01 * jax.random.normal(ks[6], (DEPTH, RET_WIDTH), jnp.float32)
    gn_b = 0.01 * jax.random.normal(ks[7], (DEPTH, RET_WIDTH), jnp.float32)
    sink = jax.random.normal(ks[8], (DEPTH, ATT_HEADS), jnp.float32)
    final_norm_w = 1.0 + 0.01 * jax.random.normal(ks[9], (D_MODEL,), jnp.float32)
    return {"x": x, "w_in": w_in, "w_out": w_out, "norm_w": norm_w,
            "decay_fwd": decay_fwd, "decay_bwd": decay_bwd, "gn_w": gn_w, "gn_b": gn_b,
            "sink": sink, "final_norm_w": final_norm_w}


def reference(x, w_in, w_out, norm_w, decay_fwd, decay_bwd, gn_w, gn_b, sink, final_norm_w):
    for l in range(DEPTH):
        x = hybrid_layer(x, w_in[l], w_out[l], norm_w[l], decay_fwd[l], decay_bwd[l], gn_w[l], gn_b[l], sink[l])
    return rms_norm(x, final_norm_w)
```

```python
import functools

import jax
import jax.numpy as jnp
from jax import lax
from jax.experimental import pallas as pl
from jax.experimental.pallas import tpu as pltpu

F32 = jnp.float32
BF16 = jnp.bfloat16

D_MODEL = 2048
HEAD_DIM = 128
RET_HEADS = 8
RET_WIDTH = RET_HEADS * HEAD_DIM
RET_CHUNK = 128
RET_THETA = 10000.0
GN_EPS = 1e-5
ATT_HEADS = 8
ATT_KV_HEADS = 2
ATT_GROUP = ATT_HEADS // ATT_KV_HEADS
ATT_WIDTH = ATT_HEADS * HEAD_DIM
ATT_KV_WIDTH = ATT_KV_HEADS * HEAD_DIM
WINDOW = 128
ROPE_THETA = 500000.0
ROPE_DIMS = HEAD_DIM // 4
NORM_EPS = 1e-6
MASK_VALUE = -1e30
D_IN_PROJ = 4 * RET_WIDTH + 2 * ATT_WIDTH + 2 * ATT_KV_WIDTH

COL_RQ = 0
COL_RK = COL_RQ + RET_HEADS
COL_RV = COL_RK + RET_HEADS
COL_RG = COL_RV + RET_HEADS
COL_AQ = COL_RG + RET_HEADS
COL_AK = COL_AQ + ATT_HEADS
COL_AV = COL_AK + ATT_KV_HEADS
COL_AG = COL_AV + ATT_KV_HEADS

IN_TM = 1024
IN_TN = 512
NORM_ROWS = 32
ATT_TQ = 512
OUT_TM = 512
VMEM_LIMIT = 56 * 1024 * 1024


def _nt_dot(a, b):
    return lax.dot_general(a, b, (((1,), (1,)), ((), ())), preferred_element_type=F32)


def _tn_dot(a, b):
    return lax.dot_general(a, b, (((0,), (0,)), ((), ())), preferred_element_type=F32)


def _in_proj_kernel(x_ref, nw_ref, w_ref, o_ref, h_ref):
    @pl.when(pl.program_id(1) == 0)
    def _():
        nw = nw_ref[...]

        def body(c, carry):
            rows = pl.ds(pl.multiple_of(c * NORM_ROWS, NORM_ROWS), NORM_ROWS)
            x = x_ref[rows, :]
            ms = jnp.mean(x * x, axis=-1, keepdims=True)
            h_ref[rows, :] = (x * lax.rsqrt(ms + NORM_EPS) * nw).astype(BF16)
            return carry

        lax.fori_loop(0, IN_TM // NORM_ROWS, body, 0)

    o_ref[...] = jnp.dot(h_ref[...], w_ref[...], preferred_element_type=F32).astype(o_ref.dtype)


def _in_proj(x2, norm_w, w_in_bf16):
    m = x2.shape[0]
    return pl.pallas_call(
        _in_proj_kernel,
        grid=(m // IN_TM, D_IN_PROJ // IN_TN),
        in_specs=[
            pl.BlockSpec((IN_TM, D_MODEL), lambda i, j: (i, 0)),
            pl.BlockSpec((1, D_MODEL), lambda i, j: (0, 0)),
            pl.BlockSpec((D_MODEL, IN_TN), lambda i, j: (0, j)),
        ],
        out_specs=pl.BlockSpec((IN_TM, IN_TN), lambda i, j: (i, j)),
        out_shape=jax.ShapeDtypeStruct((m, D_IN_PROJ), BF16),
        scratch_shapes=[pltpu.VMEM((IN_TM, D_MODEL), BF16)],
        compiler_params=pltpu.CompilerParams(
            dimension_semantics=("arbitrary", "arbitrary"), vmem_limit_bytes=VMEM_LIMIT),
        name="in_proj",
    )(x2, norm_w.reshape(1, D_MODEL), w_in_bf16)


def _log_sigmoid(z):
    return jnp.minimum(z, 0.0) - jnp.log1p(jnp.exp(-jnp.abs(z)))


def _retention_kernel(dec_ref, q_ref, k_ref, v_ref, g_ref, cos_ref, sin_ref, gnw_ref, gnb_ref,
                      o_ref, qs_ref, ks_ref, sb_ref, *, seq):
    C = RET_CHUNK
    nc = seq // C

    def rot_body(c, carry):
        rows = pl.ds(pl.multiple_of(c * C, C), C)
        cos = cos_ref[rows, :]
        sin = sin_ref[rows, :]
        q = q_ref[rows, :].astype(F32)
        k = k_ref[rows, :].astype(F32)
        qs_ref[rows, :] = (q * cos + pltpu.roll(q, HEAD_DIM // 2, 1) * sin).astype(BF16)
        kr = k * cos + pltpu.roll(k, HEAD_DIM // 2, 1) * sin
        ks_ref[rows, :] = (kr * (HEAD_DIM ** -0.5)).astype(BF16)
        return carry

    lax.fori_loop(0, nc, rot_body, 0)

    lg_f = _log_sigmoid(dec_ref[0, 0:1, :])
    lg_b = _log_sigmoid(dec_ref[0, 1:2, :])
    row = lax.broadcasted_iota(jnp.int32, (C, C), 0).astype(F32)
    col = lax.broadcasted_iota(jnp.int32, (C, C), 1).astype(F32)
    diff = row - col
    dmat = jnp.where(diff >= 0.0,
                     jnp.exp(lg_f * jnp.maximum(diff, 0.0)),
                     jnp.exp(lg_b * jnp.maximum(-diff, 0.0)))
    kdec_f = jnp.exp(lg_f * (C - 1.0 - row))
    kdec_b = jnp.exp(lg_b * row)
    qdec_f = jnp.exp(lg_f * (row + 1.0))
    qdec_b = jnp.exp(lg_b * (C - row))
    cdec_f = jnp.exp(lg_f * float(C))
    cdec_b = jnp.exp(lg_b * float(C))

    def bwd_body(t, state):
        n = nc - 1 - t
        rows = pl.ds(pl.multiple_of(n * C, C), C)
        sb_ref[n] = state.astype(BF16)
        kd = (ks_ref[rows, :].astype(F32) * kdec_b).astype(BF16)
        return state * cdec_b + _tn_dot(kd, v_ref[rows, :])

    lax.fori_loop(0, nc, bwd_body, jnp.zeros((C, C), F32))

    gnw = gnw_ref[...]
    gnb = gnb_ref[...]

    def fwd_body(n, state):
        rows = pl.ds(pl.multiple_of(n * C, C), C)
        qn = qs_ref[rows, :]
        kn = ks_ref[rows, :]
        vn = v_ref[rows, :]
        scores = _nt_dot(qn, kn) * dmat
        inner = jnp.dot(scores.astype(BF16), vn, preferred_element_type=F32)
        qf = qn.astype(F32)
        qcat = jnp.concatenate([(qf * qdec_f).astype(BF16), (qf * qdec_b).astype(BF16)], axis=1)
        scat = jnp.concatenate([state.astype(BF16), sb_ref[n]], axis=0)
        y = inner + jnp.dot(qcat, scat, preferred_element_type=F32)
        mu = jnp.mean(y, axis=-1, keepdims=True)
        yc = y - mu
        var = jnp.mean(yc * yc, axis=-1, keepdims=True)
        yn = yc * lax.rsqrt(var + GN_EPS) * gnw + gnb
        g = g_ref[rows, :].astype(F32)
        o_ref[rows, :] = (yn * (g * jax.nn.sigmoid(g))).astype(o_ref.dtype)
        kd = (kn.astype(F32) * kdec_f).astype(BF16)
        return state * cdec_f + _tn_dot(kd, vn)

    lax.fori_loop(0, nc, fwd_body, jnp.zeros((C, C), F32))


def _retention(proj, dec, cos_r, sin_r, gn_w, gn_b, batch, seq):
    blk = lambda off: pl.BlockSpec((seq, HEAD_DIM), lambda b, h: (b, off + h))
    full = pl.BlockSpec((seq, HEAD_DIM), lambda b, h: (0, 0))
    per_head = pl.BlockSpec((1, HEAD_DIM), lambda b, h: (0, h))
    return pl.pallas_call(
        functools.partial(_retention_kernel, seq=seq),
        grid=(batch, RET_HEADS),
        in_specs=[
            pl.BlockSpec((1, 8, HEAD_DIM), lambda b, h: (h, 0, 0)),
            blk(COL_RQ), blk(COL_RK), blk(COL_RV), blk(COL_RG),
            full, full, per_head, per_head,
        ],
        out_specs=pl.BlockSpec((seq, HEAD_DIM), lambda b, h: (b, h)),
        out_shape=jax.ShapeDtypeStruct((batch * seq, RET_WIDTH), BF16),
        scratch_shapes=[
            pltpu.VMEM((seq, HEAD_DIM), BF16),
            pltpu.VMEM((seq, HEAD_DIM), BF16),
            pltpu.VMEM((seq // RET_CHUNK, RET_CHUNK, HEAD_DIM), BF16),
        ],
        compiler_params=pltpu.CompilerParams(
            dimension_semantics=("arbitrary", "arbitrary"), vmem_limit_bytes=VMEM_LIMIT),
        name="retention",
    )(dec, proj, proj, proj, proj, cos_r, sin_r, gn_w.reshape(1, RET_WIDTH), gn_b.reshape(1, RET_WIDTH))


def _partial_rotary(x, cos, sin_lo, sin_hi):
    half = ROPE_DIMS // 2
    return (x * cos
            + pltpu.roll(x, HEAD_DIM - half, 1) * sin_lo
            + pltpu.roll(x, half, 1) * sin_hi)


def _attention_kernel(sink_ref, q_ref, k_ref, v_ref, g0_ref, g1_ref,
                      cosq_ref, sloq_ref, shiq_ref, cosk_ref, slok_ref, shik_ref,
                      o_ref, kr_ref, *, seq):
    T = WINDOW
    i = pl.program_id(1)

    @pl.when(i == 0)
    def _():
        def body(c, carry):
            rows = pl.ds(pl.multiple_of(c * T, T), T)
            cos, slo, shi = cosk_ref[rows, :], slok_ref[rows, :], shik_ref[rows, :]
            for kh in range(ATT_KV_HEADS):
                cols = slice(kh * HEAD_DIM, (kh + 1) * HEAD_DIM)
                k = k_ref[rows, cols].astype(F32)
                kr_ref[rows, cols] = _partial_rotary(k, cos, slo, shi).astype(BF16)
            return carry

        lax.fori_loop(0, seq // T, body, 0)

    g_refs = (g0_ref, g1_ref)
    qrow = lax.broadcasted_iota(jnp.int32, (ATT_GROUP * T, 3 * T), 0) & (T - 1)
    kcol = lax.broadcasted_iota(jnp.int32, (ATT_GROUP * T, 3 * T), 1)
    head_of_row = lax.broadcasted_iota(jnp.int32, (ATT_GROUP * T, 1), 0) // T

    for t in range(ATT_TQ // T):
        rows = slice(t * T, (t + 1) * T)
        qstart = (i * (ATT_TQ // T) + t) * T
        kstart = pl.multiple_of(jnp.clip(qstart - T, 0, seq - 3 * T), T)
        valid = jnp.abs(qrow - kcol + (qstart - kstart)) <= WINDOW
        cos, slo, shi = cosq_ref[rows, :], sloq_ref[rows, :], shiq_ref[rows, :]
        for kh in range(ATT_KV_HEADS):
            qs = []
            for g in range(ATT_GROUP):
                cols = slice((kh * ATT_GROUP + g) * HEAD_DIM, (kh * ATT_GROUP + g + 1) * HEAD_DIM)
                q = _partial_rotary(q_ref[rows, cols].astype(F32), cos, slo, shi)
                qs.append((q * (HEAD_DIM ** -0.5)).astype(BF16))
            q4 = jnp.concatenate(qs, axis=0)
            kcols = slice(kh * HEAD_DIM, (kh + 1) * HEAD_DIM)
            kw = kr_ref[pl.ds(kstart, 3 * T), kcols]
            vw = v_ref[pl.ds(kstart, 3 * T), kcols]
            s = jnp.where(valid, _nt_dot(q4, kw), MASK_VALUE)
            sink = jnp.zeros((ATT_GROUP * T, 1), F32)
            for g in range(ATT_GROUP):
                sink = jnp.where(head_of_row == g, sink_ref[kh * ATT_GROUP + g], sink)
            m = jnp.maximum(jnp.max(s, axis=-1, keepdims=True), sink)
            p = jnp.exp(s - m)
            denom = jnp.sum(p, axis=-1, keepdims=True) + jnp.exp(sink - m)
            out = jnp.dot(p.astype(BF16), vw, preferred_element_type=F32) / denom
            for g in range(ATT_GROUP):
                gcols = slice(g * HEAD_DIM, (g + 1) * HEAD_DIM)
                gate = g_refs[kh][rows, gcols].astype(F32)
                ocols = slice((kh * ATT_GROUP + g) * HEAD_DIM, (kh * ATT_GROUP + g + 1) * HEAD_DIM)
                o_ref[rows, ocols] = (out[g * T:(g + 1) * T] * (gate * jax.nn.sigmoid(gate))).astype(o_ref.dtype)


def _attention(proj, sink, tabs, batch, seq):
    nq = seq // ATT_TQ
    gw = ATT_GROUP * HEAD_DIM
    qtab = pl.BlockSpec((ATT_TQ, HEAD_DIM), lambda b, i: (i, 0))
    ktab = pl.BlockSpec((seq, HEAD_DIM), lambda b, i: (0, 0))
    return pl.pallas_call(
        functools.partial(_attention_kernel, seq=seq),
        grid=(batch, nq),
        in_specs=[
            pl.BlockSpec(memory_space=pltpu.SMEM),
            pl.BlockSpec((ATT_TQ, ATT_WIDTH), lambda b, i: (b * nq + i, COL_AQ * HEAD_DIM // ATT_WIDTH)),
            pl.BlockSpec((seq, ATT_KV_WIDTH), lambda b, i: (b, COL_AK * HEAD_DIM // ATT_KV_WIDTH)),
            pl.BlockSpec((seq, ATT_KV_WIDTH), lambda b, i: (b, COL_AV * HEAD_DIM // ATT_KV_WIDTH)),
            pl.BlockSpec((ATT_TQ, gw), lambda b, i: (b * nq + i, COL_AG * HEAD_DIM // gw)),
            pl.BlockSpec((ATT_TQ, gw), lambda b, i: (b * nq + i, COL_AG * HEAD_DIM // gw + 1)),
            qtab, qtab, qtab, ktab, ktab, ktab,
        ],
        out_specs=pl.BlockSpec((ATT_TQ, ATT_WIDTH), lambda b, i: (b * nq + i, 0)),
        out_shape=jax.ShapeDtypeStruct((batch * seq, ATT_WIDTH), BF16),
        scratch_shapes=[pltpu.VMEM((seq, ATT_KV_WIDTH), BF16)],
        compiler_params=pltpu.CompilerParams(
            dimension_semantics=("arbitrary", "arbitrary"), vmem_limit_bytes=VMEM_LIMIT),
        name="attention",
    )(sink, proj, proj, proj, proj, proj, *tabs, *tabs)


def _out_proj_kernel(x_ref, yr_ref, ya_ref, wr_ref, wa_ref, fw_ref, o_ref):
    acc = jnp.dot(yr_ref[...], wr_ref[...], preferred_element_type=F32)
    acc = acc + jnp.dot(ya_ref[...], wa_ref[...], preferred_element_type=F32)
    z = x_ref[...] + acc
    ms = jnp.mean(z * z, axis=-1, keepdims=True)
    o_ref[...] = z * lax.rsqrt(ms + NORM_EPS) * fw_ref[...]


def _out_proj(x2, y_ret, y_att, w_out_bf16, final_norm_w):
    m = x2.shape[0]
    return pl.pallas_call(
        _out_proj_kernel,
        grid=(m // OUT_TM,),
        in_specs=[
            pl.BlockSpec((OUT_TM, D_MODEL), lambda i: (i, 0)),
            pl.BlockSpec((OUT_TM, RET_WIDTH), lambda i: (i, 0)),
            pl.BlockSpec((OUT_TM, ATT_WIDTH), lambda i: (i, 0)),
            pl.BlockSpec((RET_WIDTH, D_MODEL), lambda i: (0, 0)),
            pl.BlockSpec((ATT_WIDTH, D_MODEL), lambda i: (RET_WIDTH // ATT_WIDTH, 0)),
            pl.BlockSpec((1, D_MODEL), lambda i: (0, 0)),
        ],
        out_specs=pl.BlockSpec((OUT_TM, D_MODEL), lambda i: (i, 0)),
        out_shape=jax.ShapeDtypeStruct((m, D_MODEL), F32),
        compiler_params=pltpu.CompilerParams(
            dimension_semantics=("arbitrary",), vmem_limit_bytes=VMEM_LIMIT),
        name="out_proj",
    )(x2, y_ret, y_att, w_out_bf16, w_out_bf16, final_norm_w.reshape(1, D_MODEL))


def _retention_tables(seq):
    half = HEAD_DIM // 2
    inv_freq = RET_THETA ** (-jnp.arange(half, dtype=F32) / half)
    ang = jnp.arange(seq, dtype=F32)[:, None] * inv_freq[None, :]
    cos, sin = jnp.cos(ang), jnp.sin(ang)
    return jnp.concatenate([cos, cos], axis=1), jnp.concatenate([-sin, sin], axis=1)


def _attention_tables(seq):
    half = ROPE_DIMS // 2
    inv_freq = ROPE_THETA ** (-jnp.arange(half, dtype=F32) / half)
    ang = jnp.arange(seq, dtype=F32)[:, None] * inv_freq[None, :]
    cos, sin = jnp.cos(ang), jnp.sin(ang)
    rest = HEAD_DIM - ROPE_DIMS
    zeros = jnp.zeros((seq, half), F32)
    pad = jnp.zeros((seq, rest), F32)
    cos_t = jnp.concatenate([cos, cos, jnp.ones((seq, rest), F32)], axis=1)
    sin_lo = jnp.concatenate([-sin, zeros, pad], axis=1)
    sin_hi = jnp.concatenate([zeros, sin, pad], axis=1)
    return cos_t, sin_lo, sin_hi


def _layer(x2, w_in, w_out, norm_w, decay_fwd, decay_bwd, gn_w, gn_b, sink, post_norm_w, batch, seq):
    proj = _in_proj(x2, norm_w, w_in.astype(BF16))
    dec = jnp.zeros((RET_HEADS, 8, HEAD_DIM), F32)
    dec = dec.at[:, 0, :].set(decay_fwd.astype(F32)[:, None])
    dec = dec.at[:, 1, :].set(decay_bwd.astype(F32)[:, None])
    y_ret = _retention(proj, dec, *_retention_tables(seq), gn_w.astype(F32), gn_b.astype(F32), batch, seq)
    y_att = _attention(proj, sink.astype(F32), _attention_tables(seq), batch, seq)
    return _out_proj(x2, y_ret, y_att, w_out.astype(BF16), post_norm_w)


def kernel(x, w_in, w_out, norm_w, decay_fwd, decay_bwd, gn_w, gn_b, sink, final_norm_w):
    batch, seq, _ = x.shape
    depth = w_in.shape[0]
    assert depth == 1, "the final RMSNorm is fused into the single layer's output projection"
    x2 = x.reshape(batch * seq, D_MODEL)
    out = _layer(x2, w_in[0], w_out[0], norm_w[0], decay_fwd[0], decay_bwd[0], gn_w[0], gn_b[0], sink[0],
                 final_norm_w, batch, seq)
    return out.reshape(batch, seq, D_MODEL)
```

```python
import functools

import jax
import jax.numpy as jnp
from jax import lax
from jax.experimental import pallas as pl
from jax.experimental.pallas import tpu as pltpu

F32 = jnp.float32
BF16 = jnp.bfloat16

D_MODEL = 2048
HEAD_DIM = 128
RET_HEADS = 8
RET_WIDTH = RET_HEADS * HEAD_DIM
RET_CHUNK = 128
RET_THETA = 10000.0
GN_EPS = 1e-5
ATT_HEADS = 8
ATT_KV_HEADS = 2
ATT_GROUP = ATT_HEADS // ATT_KV_HEADS
ATT_WIDTH = ATT_HEADS * HEAD_DIM
ATT_KV_WIDTH = ATT_KV_HEADS * HEAD_DIM
WINDOW = 128
ROPE_THETA = 500000.0
ROPE_DIMS = HEAD_DIM // 4
NORM_EPS = 1e-6
MASK_VALUE = -1e30
LOG2E = 1.4426950408889634
D_IN_PROJ = 4 * RET_WIDTH + 2 * ATT_WIDTH + 2 * ATT_KV_WIDTH

COL_RQ = 0
COL_RK = COL_RQ + RET_HEADS
COL_RV = COL_RK + RET_HEADS
COL_RG = COL_RV + RET_HEADS
COL_AQ = COL_RG + RET_HEADS
COL_AK = COL_AQ + ATT_HEADS
COL_AV = COL_AK + ATT_KV_HEADS
COL_AG = COL_AV + ATT_KV_HEADS

IN_TM = 1024
IN_TN = 1664
NORM_ROWS = 128
RET_UNROLL = 8
ATT_TQ = 512
OUT_TM = 512
VMEM_LIMIT = 56 * 1024 * 1024


def _nt_dot(a, b):
    return lax.dot_general(a, b, (((1,), (1,)), ((), ())), preferred_element_type=F32)


def _tn_dot(a, b):
    return lax.dot_general(a, b, (((0,), (0,)), ((), ())), preferred_element_type=F32)


def _in_proj_kernel(x_ref, nw_ref, w_ref, o_ref, h_ref):
    @pl.when(pl.program_id(1) == 0)
    def _():
        nw = nw_ref[...]

        def body(c, carry):
            rows = pl.ds(pl.multiple_of(c * NORM_ROWS, NORM_ROWS), NORM_ROWS)
            x = x_ref[rows, :]
            ms = jnp.mean(x * x, axis=-1, keepdims=True)
            h_ref[rows, :] = (x * lax.rsqrt(ms + NORM_EPS) * nw).astype(BF16)
            return carry

        lax.fori_loop(0, IN_TM // NORM_ROWS, body, 0)

    o_ref[...] = jnp.dot(h_ref[...], w_ref[...], preferred_element_type=F32).astype(o_ref.dtype)


def _in_proj(x2, norm_w, w_in_bf16):
    m = x2.shape[0]
    return pl.pallas_call(
        _in_proj_kernel,
        grid=(m // IN_TM, D_IN_PROJ // IN_TN),
        in_specs=[
            pl.BlockSpec((IN_TM, D_MODEL), lambda i, j: (i, 0)),
            pl.BlockSpec((1, D_MODEL), lambda i, j: (0, 0)),
            pl.BlockSpec((D_MODEL, IN_TN), lambda i, j: (0, j)),
        ],
        out_specs=pl.BlockSpec((IN_TM, IN_TN), lambda i, j: (i, j)),
        out_shape=jax.ShapeDtypeStruct((m, D_IN_PROJ), BF16),
        scratch_shapes=[pltpu.VMEM((IN_TM, D_MODEL), BF16)],
        compiler_params=pltpu.CompilerParams(
            dimension_semantics=("arbitrary", "arbitrary"), vmem_limit_bytes=VMEM_LIMIT),
        name="in_proj",
    )(x2, norm_w.reshape(1, D_MODEL), w_in_bf16)


def _log_sigmoid(z):
    return jnp.minimum(z, 0.0) - jnp.log1p(jnp.exp(-jnp.abs(z)))


def _retention_kernel(dec_ref, q_ref, k_ref, v_ref, g_ref, cos_ref, sin_ref, gnw_ref, gnb_ref,
                      o_ref, ks_ref, kv_ref, st_ref, *, seq):
    C = RET_CHUNK
    nc = seq // C
    U = RET_UNROLL

    lg_f = _log_sigmoid(dec_ref[0, 0:1, :])
    lg_b = _log_sigmoid(dec_ref[0, 1:2, :])
    row = lax.broadcasted_iota(jnp.int32, (C, C), 0).astype(F32)
    col = lax.broadcasted_iota(jnp.int32, (C, C), 1).astype(F32)
    diff = row - col
    dmat = jnp.where(diff >= 0.0,
                     jnp.exp(lg_f * jnp.maximum(diff, 0.0)),
                     jnp.exp(lg_b * jnp.maximum(-diff, 0.0)))
    kdec_f = jnp.exp(lg_f * (C - 1.0 - row))
    kdec_b = jnp.exp(lg_b * row)
    qdec_f = jnp.exp(lg_f * (row + 1.0))
    qdec_b = jnp.exp(lg_b * (C - row))
    cdec_f = jnp.exp(lg_f * float(C))
    cdec_b = jnp.exp(lg_b * float(C))

    def chunk_rows(n):
        return pl.ds(pl.multiple_of(n * C, C), C)

    def rotary(ref, rows):
        x = ref[rows, :].astype(F32)
        return x * cos_ref[rows, :] + pltpu.roll(x, HEAD_DIM // 2, 1) * sin_ref[rows, :]

    def kv_body(it, carry):
        for u in range(U):
            n = it * U + u
            rows = chunk_rows(n)
            kr = rotary(k_ref, rows) * (HEAD_DIM ** -0.5)
            ks_ref[rows, :] = kr.astype(BF16)
            kd = jnp.concatenate([(kr * kdec_f).astype(BF16), (kr * kdec_b).astype(BF16)], axis=1)
            kv_ref[n] = _tn_dot(kd, v_ref[rows, :])
        return carry

    lax.fori_loop(0, nc // U, kv_body, 0)

    def scan_body(t, carry):
        sf, sb = carry
        m = nc - 1 - t
        st_ref[t, 0:HEAD_DIM, :] = sf.astype(BF16)
        st_ref[m, HEAD_DIM:2 * HEAD_DIM, :] = sb.astype(BF16)
        sf = sf * cdec_f + kv_ref[t, 0:HEAD_DIM, :]
        sb = sb * cdec_b + kv_ref[m, HEAD_DIM:2 * HEAD_DIM, :]
        return sf, sb

    zero = jnp.zeros((HEAD_DIM, HEAD_DIM), F32)
    lax.fori_loop(0, nc, scan_body, (zero, zero))

    gnw = gnw_ref[...]
    gnb = gnb_ref[...]

    def out_body(it, carry):
        for u in range(U):
            n = it * U + u
            rows = chunk_rows(n)
            qr = rotary(q_ref, rows)
            vn = v_ref[rows, :]
            scores = _nt_dot(qr.astype(BF16), ks_ref[rows, :]) * dmat
            inner = jnp.dot(scores.astype(BF16), vn, preferred_element_type=F32)
            qcat = jnp.concatenate([(qr * qdec_f).astype(BF16), (qr * qdec_b).astype(BF16)], axis=1)
            y = inner + jnp.dot(qcat, st_ref[n], preferred_element_type=F32)
            mu = jnp.mean(y, axis=-1, keepdims=True)
            yc = y - mu
            var = jnp.mean(yc * yc, axis=-1, keepdims=True)
            yn = yc * lax.rsqrt(var + GN_EPS) * gnw + gnb
            g = g_ref[rows, :].astype(F32)
            o_ref[rows, :] = (yn * (g * jax.nn.sigmoid(g))).astype(o_ref.dtype)
        return carry

    lax.fori_loop(0, nc // U, out_body, 0)


def _retention(proj, dec, cos_r, sin_r, gn_w, gn_b, batch, seq):
    blk = lambda off: pl.BlockSpec((seq, HEAD_DIM), lambda b, h: (b, off + h))
    full = pl.BlockSpec((seq, HEAD_DIM), lambda b, h: (0, 0))
    per_head = pl.BlockSpec((1, HEAD_DIM), lambda b, h: (0, h))
    return pl.pallas_call(
        functools.partial(_retention_kernel, seq=seq),
        grid=(batch, RET_HEADS),
        in_specs=[
            pl.BlockSpec((1, 8, HEAD_DIM), lambda b, h: (h, 0, 0)),
            blk(COL_RQ), blk(COL_RK), blk(COL_RV), blk(COL_RG),
            full, full, per_head, per_head,
        ],
        out_specs=pl.BlockSpec((seq, HEAD_DIM), lambda b, h: (b, h)),
        out_shape=jax.ShapeDtypeStruct((batch * seq, RET_WIDTH), BF16),
        scratch_shapes=[
            pltpu.VMEM((seq, HEAD_DIM), BF16),
            pltpu.VMEM((seq // RET_CHUNK, 2 * HEAD_DIM, HEAD_DIM), F32),
            pltpu.VMEM((seq // RET_CHUNK, 2 * HEAD_DIM, HEAD_DIM), BF16),
        ],
        compiler_params=pltpu.CompilerParams(
            dimension_semantics=("arbitrary", "arbitrary"), vmem_limit_bytes=VMEM_LIMIT),
        name="retention",
    )(dec, proj, proj, proj, proj, cos_r, sin_r, gn_w.reshape(1, RET_WIDTH), gn_b.reshape(1, RET_WIDTH))


def _partial_rotary(x, cos, sin_lo, sin_hi):
    half = ROPE_DIMS // 2
    return (x * cos
            + pltpu.roll(x, HEAD_DIM - half, 1) * sin_lo
            + pltpu.roll(x, half, 1) * sin_hi)


def _attention_kernel(sink_ref, q_ref, k_ref, v_ref, g0_ref, g1_ref,
                      cosq_ref, sloq_ref, shiq_ref, cosk_ref, slok_ref, shik_ref,
                      o_ref, kr_ref, vx_ref, *, seq):
    T = WINDOW
    i = pl.program_id(1)

    @pl.when(i == 0)
    def _():
        ones = jnp.ones((T, HEAD_DIM), BF16)

        def body(c, carry):
            rows = pl.ds(pl.multiple_of(c * T, T), T)
            cos, slo, shi = cosk_ref[rows, :], slok_ref[rows, :], shik_ref[rows, :]
            for kh in range(ATT_KV_HEADS):
                cols = slice(kh * HEAD_DIM, (kh + 1) * HEAD_DIM)
                k = k_ref[rows, cols].astype(F32)
                kr_ref[rows, cols] = _partial_rotary(k, cos, slo, shi).astype(BF16)
                vx_ref[rows, 2 * kh * HEAD_DIM:(2 * kh + 1) * HEAD_DIM] = v_ref[rows, cols]
                vx_ref[rows, (2 * kh + 1) * HEAD_DIM:(2 * kh + 2) * HEAD_DIM] = ones
            return carry

        lax.fori_loop(0, seq // T, body, 0)

    g_refs = (g0_ref, g1_ref)
    qrow = lax.broadcasted_iota(jnp.int32, (T, 3 * T), 0)
    kcol = lax.broadcasted_iota(jnp.int32, (T, 3 * T), 1)
    qscale = (HEAD_DIM ** -0.5) * LOG2E

    for t in range(ATT_TQ // T):
        rows = slice(t * T, (t + 1) * T)
        qstart = (i * (ATT_TQ // T) + t) * T
        kstart = pl.multiple_of(jnp.clip(qstart - T, 0, seq - 3 * T), T)
        bias = jnp.where(jnp.abs(qrow - kcol + (qstart - kstart)) <= WINDOW, 0.0, MASK_VALUE)
        cos, slo, shi = cosq_ref[rows, :], sloq_ref[rows, :], shiq_ref[rows, :]
        for kh in range(ATT_KV_HEADS):
            qs = []
            for g in range(ATT_GROUP):
                cols = slice((kh * ATT_GROUP + g) * HEAD_DIM, (kh * ATT_GROUP + g + 1) * HEAD_DIM)
                q = _partial_rotary(q_ref[rows, cols].astype(F32), cos, slo, shi)
                qs.append((q * qscale).astype(BF16))
            q4 = jnp.concatenate(qs, axis=0)
            kw = kr_ref[pl.ds(kstart, 3 * T), kh * HEAD_DIM:(kh + 1) * HEAD_DIM]
            vw = vx_ref[pl.ds(kstart, 3 * T), 2 * kh * HEAD_DIM:(2 * kh + 2) * HEAD_DIM]
            s4 = _nt_dot(q4, kw)
            ps, ms, sinks = [], [], []
            for g in range(ATT_GROUP):
                s = s4[g * T:(g + 1) * T] + bias
                sink = sink_ref[kh * ATT_GROUP + g] * LOG2E
                m = jnp.maximum(jnp.max(s, axis=-1, keepdims=True), sink)
                ps.append(jnp.exp2(s - m).astype(BF16))
                ms.append(m)
                sinks.append(sink)
            pv = jnp.dot(jnp.concatenate(ps, axis=0), vw, preferred_element_type=F32)
            for g in range(ATT_GROUP):
                blk = pv[g * T:(g + 1) * T]
                denom = blk[:, HEAD_DIM:HEAD_DIM + 1] + jnp.exp2(sinks[g] - ms[g])
                gate = g_refs[kh][rows, g * HEAD_DIM:(g + 1) * HEAD_DIM].astype(F32)
                ocols = slice((kh * ATT_GROUP + g) * HEAD_DIM, (kh * ATT_GROUP + g + 1) * HEAD_DIM)
                o_ref[rows, ocols] = (blk[:, :HEAD_DIM] / denom * (gate * jax.nn.sigmoid(gate))).astype(o_ref.dtype)


def _attention(proj, sink, tabs, batch, seq):
    nq = seq // ATT_TQ
    gw = ATT_GROUP * HEAD_DIM
    qtab = pl.BlockSpec((ATT_TQ, HEAD_DIM), lambda b, i: (i, 0))
    ktab = pl.BlockSpec((seq, HEAD_DIM), lambda b, i: (0, 0))
    return pl.pallas_call(
        functools.partial(_attention_kernel, seq=seq),
        grid=(batch, nq),
        in_specs=[
            pl.BlockSpec(memory_space=pltpu.SMEM),
            pl.BlockSpec((ATT_TQ, ATT_WIDTH), lambda b, i: (b * nq + i, COL_AQ * HEAD_DIM // ATT_WIDTH)),
            pl.BlockSpec((seq, ATT_KV_WIDTH), lambda b, i: (b, COL_AK * HEAD_DIM // ATT_KV_WIDTH)),
            pl.BlockSpec((seq, ATT_KV_WIDTH), lambda b, i: (b, COL_AV * HEAD_DIM // ATT_KV_WIDTH)),
            pl.BlockSpec((ATT_TQ, gw), lambda b, i: (b * nq + i, COL_AG * HEAD_DIM // gw)),
            pl.BlockSpec((ATT_TQ, gw), lambda b, i: (b * nq + i, COL_AG * HEAD_DIM // gw + 1)),
            qtab, qtab, qtab, ktab, ktab, ktab,
        ],
        out_specs=pl.BlockSpec((ATT_TQ, ATT_WIDTH), lambda b, i: (b * nq + i, 0)),
        out_shape=jax.ShapeDtypeStruct((batch * seq, ATT_WIDTH), BF16),
        scratch_shapes=[pltpu.VMEM((seq, ATT_KV_WIDTH), BF16), pltpu.VMEM((seq, 2 * ATT_KV_WIDTH), BF16)],
        compiler_params=pltpu.CompilerParams(
            dimension_semantics=("arbitrary", "arbitrary"), vmem_limit_bytes=VMEM_LIMIT),
        name="attention",
    )(sink, proj, proj, proj, proj, proj, *tabs, *tabs)


def _out_proj_kernel(x_ref, yr_ref, ya_ref, wr_ref, wa_ref, fw_ref, o_ref):
    acc = jnp.dot(yr_ref[...], wr_ref[...], preferred_element_type=F32)
    acc = acc + jnp.dot(ya_ref[...], wa_ref[...], preferred_element_type=F32)
    z = x_ref[...] + acc
    ms = jnp.mean(z * z, axis=-1, keepdims=True)
    o_ref[...] = z * lax.rsqrt(ms + NORM_EPS) * fw_ref[...]


def _out_proj(x2, y_ret, y_att, w_out_bf16, final_norm_w):
    m = x2.shape[0]
    return pl.pallas_call(
        _out_proj_kernel,
        grid=(m // OUT_TM,),
        in_specs=[
            pl.BlockSpec((OUT_TM, D_MODEL), lambda i: (i, 0)),
            pl.BlockSpec((OUT_TM, RET_WIDTH), lambda i: (i, 0)),
            pl.BlockSpec((OUT_TM, ATT_WIDTH), lambda i: (i, 0)),
            pl.BlockSpec((RET_WIDTH, D_MODEL), lambda i: (0, 0)),
            pl.BlockSpec((ATT_WIDTH, D_MODEL), lambda i: (RET_WIDTH // ATT_WIDTH, 0)),
            pl.BlockSpec((1, D_MODEL), lambda i: (0, 0)),
        ],
        out_specs=pl.BlockSpec((OUT_TM, D_MODEL), lambda i: (i, 0)),
        out_shape=jax.ShapeDtypeStruct((m, D_MODEL), F32),
        compiler_params=pltpu.CompilerParams(
            dimension_semantics=("arbitrary",), vmem_limit_bytes=VMEM_LIMIT),
        name="out_proj",
    )(x2, y_ret, y_att, w_out_bf16, w_out_bf16, final_norm_w.reshape(1, D_MODEL))


def _retention_tables(seq):
    half = HEAD_DIM // 2
    inv_freq = RET_THETA ** (-jnp.arange(half, dtype=F32) / half)
    ang = jnp.arange(seq, dtype=F32)[:, None] * inv_freq[None, :]
    cos, sin = jnp.cos(ang), jnp.sin(ang)
    return jnp.concatenate([cos, cos], axis=1), jnp.concatenate([-sin, sin], axis=1)


def _attention_tables(seq):
    half = ROPE_DIMS // 2
    inv_freq = ROPE_THETA ** (-jnp.arange(half, dtype=F32) / half)
    ang = jnp.arange(seq, dtype=F32)[:, None] * inv_freq[None, :]
    cos, sin = jnp.cos(ang), jnp.sin(ang)
    rest = HEAD_DIM - ROPE_DIMS
    zeros = jnp.zeros((seq, half), F32)
    pad = jnp.zeros((seq, rest), F32)
    cos_t = jnp.concatenate([cos, cos, jnp.ones((seq, rest), F32)], axis=1)
    sin_lo = jnp.concatenate([-sin, zeros, pad], axis=1)
    sin_hi = jnp.concatenate([zeros, sin, pad], axis=1)
    return cos_t, sin_lo, sin_hi


def _layer(x2, w_in, w_out, norm_w, decay_fwd, decay_bwd, gn_w, gn_b, sink, post_norm_w, batch, seq):
    proj = _in_proj(x2, norm_w, w_in.astype(BF16))
    dec = jnp.zeros((RET_HEADS, 8, HEAD_DIM), F32)
    dec = dec.at[:, 0, :].set(decay_fwd.astype(F32)[:, None])
    dec = dec.at[:, 1, :].set(decay_bwd.astype(F32)[:, None])
    y_ret = _retention(proj, dec, *_retention_tables(seq), gn_w.astype(F32), gn_b.astype(F32), batch, seq)
    y_att = _attention(proj, sink.astype(F32), _attention_tables(seq), batch, seq)
    return _out_proj(x2, y_ret, y_att, w_out.astype(BF16), post_norm_w)


def kernel(x, w_in, w_out, norm_w, decay_fwd, decay_bwd, gn_w, gn_b, sink, final_norm_w):
    batch, seq, _ = x.shape
    depth = w_in.shape[0]
    assert depth == 1, "the final RMSNorm is fused into the single layer's output projection"
    x2 = x.reshape(batch * seq, D_MODEL)
    out = _layer(x2, w_in[0], w_out[0], norm_w[0], decay_fwd[0], decay_bwd[0], gn_w[0], gn_b[0], sink[0],
                 final_norm_w, batch, seq)
    return out.reshape(batch, seq, D_MODEL)
```

```python
import functools

import jax
import jax.numpy as jnp
from jax import lax
from jax.experimental import pallas as pl
from jax.experimental.pallas import tpu as pltpu

F32 = jnp.float32
BF16 = jnp.bfloat16

D_MODEL = 2048
HEAD_DIM = 128
RET_HEADS = 8
RET_WIDTH = RET_HEADS * HEAD_DIM
RET_CHUNK = 128
RET_THETA = 10000.0
GN_EPS = 1e-5
ATT_HEADS = 8
ATT_KV_HEADS = 2
ATT_GROUP = ATT_HEADS // ATT_KV_HEADS
ATT_WIDTH = ATT_HEADS * HEAD_DIM
ATT_KV_WIDTH = ATT_KV_HEADS * HEAD_DIM
WINDOW = 128
ROPE_THETA = 500000.0
ROPE_DIMS = HEAD_DIM // 4
NORM_EPS = 1e-6
MASK_VALUE = -1e30
LOG2E = 1.4426950408889634
D_IN_PROJ = 4 * RET_WIDTH + 2 * ATT_WIDTH + 2 * ATT_KV_WIDTH

COL_RQ = 0
COL_RK = COL_RQ + RET_HEADS
COL_RV = COL_RK + RET_HEADS
COL_RG = COL_RV + RET_HEADS
COL_AQ = COL_RG + RET_HEADS
COL_AK = COL_AQ + ATT_HEADS
COL_AV = COL_AK + ATT_KV_HEADS
COL_AG = COL_AV + ATT_KV_HEADS
N_COL_GROUPS = COL_AG + ATT_HEADS

PLAIN, RET_ROT, ATT_ROT_Q, ATT_ROT_K, SILU = range(5)
GROUP_KIND = ([RET_ROT] * (2 * RET_HEADS) + [PLAIN] * RET_HEADS + [SILU] * RET_HEADS
              + [ATT_ROT_Q] * ATT_HEADS + [ATT_ROT_K] * ATT_KV_HEADS + [PLAIN] * ATT_KV_HEADS + [SILU] * ATT_HEADS)
assert len(GROUP_KIND) == N_COL_GROUPS
TAB_RET_COS, TAB_RET_SIN, TAB_ATT_Q, TAB_ATT_K, N_TABLES = 0, 1, 2, 5, 8

IN_TM = 256
IN_CHUNK_GROUPS = 4
NORM_ROWS = 128
RET_UNROLL = 8
ATT_TQ = 512
OUT_TM = 512
VMEM_LIMIT = 56 * 1024 * 1024


def _nt_dot(a, b):
    return lax.dot_general(a, b, (((1,), (1,)), ((), ())), preferred_element_type=F32)


def _tn_dot(a, b):
    return lax.dot_general(a, b, (((0,), (0,)), ((), ())), preferred_element_type=F32)


def _partial_rotary(x, cos, sin_lo, sin_hi):
    half = ROPE_DIMS // 2
    return (x * cos
            + pltpu.roll(x, HEAD_DIM - half, 1) * sin_lo
            + pltpu.roll(x, half, 1) * sin_hi)


def _in_proj_kernel(x_ref, nw_ref, w_ref, tab_ref, ret_ref, aq_ref, ak_ref, av_ref, ag_ref, h_ref):
    nw = nw_ref[...]

    def store(g, a):
        a = a.astype(BF16)
        if g < COL_AQ:
            ret_ref[g] = a
        else:
            for ref, lo, hi in ((aq_ref, COL_AQ, COL_AK), (ak_ref, COL_AK, COL_AV),
                                (av_ref, COL_AV, COL_AG), (ag_ref, COL_AG, N_COL_GROUPS)):
                if lo <= g < hi:
                    ref[:, (g - lo) * HEAD_DIM:(g - lo + 1) * HEAD_DIM] = a

    def norm_body(c, carry):
        rows = pl.ds(pl.multiple_of(c * NORM_ROWS, NORM_ROWS), NORM_ROWS)
        x = x_ref[rows, :]
        ms = jnp.mean(x * x, axis=-1, keepdims=True)
        h_ref[rows, :] = (x * lax.rsqrt(ms + NORM_EPS) * nw).astype(BF16)
        return carry

    lax.fori_loop(0, IN_TM // NORM_ROWS, norm_body, 0)

    h = h_ref[...]
    cw = IN_CHUNK_GROUPS * HEAD_DIM
    for c in range(N_COL_GROUPS // IN_CHUNK_GROUPS):
        acc = jnp.dot(h, w_ref[:, c * cw:(c + 1) * cw], preferred_element_type=F32)
        for u in range(IN_CHUNK_GROUPS):
            g = c * IN_CHUNK_GROUPS + u
            a = acc[:, u * HEAD_DIM:(u + 1) * HEAD_DIM]
            kind = GROUP_KIND[g]
            if kind == RET_ROT:
                a = a * tab_ref[TAB_RET_COS] + pltpu.roll(a, HEAD_DIM // 2, 1) * tab_ref[TAB_RET_SIN]
            elif kind == ATT_ROT_Q:
                a = _partial_rotary(a, *(tab_ref[TAB_ATT_Q + t] for t in range(3)))
            elif kind == ATT_ROT_K:
                a = _partial_rotary(a, *(tab_ref[TAB_ATT_K + t] for t in range(3)))
            elif kind == SILU:
                a = a * jax.nn.sigmoid(a)
            store(g, a)


def _in_proj(x2, norm_w, w_in_bf16, tables, seq):
    m = x2.shape[0]
    tiles_per_seq = seq // IN_TM
    assert tables.shape == (N_TABLES, seq, HEAD_DIM) and seq % IN_TM == 0 and m % seq == 0
    tok = lambda width: pl.BlockSpec((IN_TM, width), lambda i: (i, 0))
    tok_shape = lambda width: jax.ShapeDtypeStruct((m, width), BF16)
    return pl.pallas_call(
        _in_proj_kernel,
        grid=(m // IN_TM,),
        in_specs=[
            pl.BlockSpec((IN_TM, D_MODEL), lambda i: (i, 0)),
            pl.BlockSpec((1, D_MODEL), lambda i: (0, 0)),
            pl.BlockSpec((D_MODEL, D_IN_PROJ), lambda i: (0, 0), pipeline_mode=pl.Buffered(1)),
            pl.BlockSpec((N_TABLES, IN_TM, HEAD_DIM), lambda i: (0, i % tiles_per_seq, 0)),
        ],
        out_specs=[pl.BlockSpec((COL_AQ, IN_TM, HEAD_DIM), lambda i: (0, i, 0)),
                   tok(ATT_WIDTH), tok(ATT_KV_WIDTH), tok(ATT_KV_WIDTH), tok(ATT_WIDTH)],
        out_shape=[jax.ShapeDtypeStruct((COL_AQ, m, HEAD_DIM), BF16),
                   tok_shape(ATT_WIDTH), tok_shape(ATT_KV_WIDTH), tok_shape(ATT_KV_WIDTH), tok_shape(ATT_WIDTH)],
        scratch_shapes=[pltpu.VMEM((IN_TM, D_MODEL), BF16)],
        compiler_params=pltpu.CompilerParams(dimension_semantics=("arbitrary",), vmem_limit_bytes=VMEM_LIMIT),
        name="in_proj",
    )(x2, norm_w.reshape(1, D_MODEL), w_in_bf16, tables)


def _log_sigmoid(z):
    return jnp.minimum(z, 0.0) - jnp.log1p(jnp.exp(-jnp.abs(z)))


def _retention_kernel(dec_ref, q_ref, k_ref, v_ref, g_ref, gnw_ref, gnb_ref, o_ref, kv_ref, st_ref, *, seq):
    C = RET_CHUNK
    nc = seq // C
    U = RET_UNROLL
    scale = HEAD_DIM ** -0.5

    lg_f = _log_sigmoid(dec_ref[0, 0:1, :])
    lg_b = _log_sigmoid(dec_ref[0, 1:2, :])
    row = lax.broadcasted_iota(jnp.int32, (C, C), 0).astype(F32)
    col = lax.broadcasted_iota(jnp.int32, (C, C), 1).astype(F32)
    diff = row - col
    dmat = scale * jnp.where(diff >= 0.0,
                             jnp.exp(lg_f * jnp.maximum(diff, 0.0)),
                             jnp.exp(lg_b * jnp.maximum(-diff, 0.0)))
    kdec_f = (scale * jnp.exp(lg_f * (C - 1.0 - row))).astype(BF16)
    kdec_b = (scale * jnp.exp(lg_b * row)).astype(BF16)
    qdec_f = jnp.exp(lg_f * (row + 1.0)).astype(BF16)
    qdec_b = jnp.exp(lg_b * (C - row)).astype(BF16)
    cdec_f = jnp.exp(lg_f * float(C))
    cdec_b = jnp.exp(lg_b * float(C))

    def chunk_rows(n):
        return pl.ds(pl.multiple_of(n * C, C), C)

    def kv_body(it, carry):
        for u in range(U):
            n = it * U + u
            rows = chunk_rows(n)
            k = k_ref[0, rows, :]
            kd = jnp.concatenate([k * kdec_f, k * kdec_b], axis=1)
            kv_ref[n] = _tn_dot(kd, v_ref[0, rows, :])
        return carry

    lax.fori_loop(0, nc // U, kv_body, 0)

    def scan_body(t, carry):
        sf, sb = carry
        m = nc - 1 - t
        st_ref[t, 0:HEAD_DIM, :] = sf.astype(BF16)
        st_ref[m, HEAD_DIM:2 * HEAD_DIM, :] = sb.astype(BF16)
        sf = sf * cdec_f + kv_ref[t, 0:HEAD_DIM, :]
        sb = sb * cdec_b + kv_ref[m, HEAD_DIM:2 * HEAD_DIM, :]
        return sf, sb

    zero = jnp.zeros((HEAD_DIM, HEAD_DIM), F32)
    lax.fori_loop(0, nc, scan_body, (zero, zero))

    gnw = gnw_ref[...]
    gnb = gnb_ref[...]

    def out_body(it, carry):
        for u in range(U):
            n = it * U + u
            rows = chunk_rows(n)
            q = q_ref[0, rows, :]
            vn = v_ref[0, rows, :]
            scores = _nt_dot(q, k_ref[0, rows, :]) * dmat
            inner = jnp.dot(scores.astype(BF16), vn, preferred_element_type=F32)
            qcat = jnp.concatenate([q * qdec_f, q * qdec_b], axis=1)
            y = inner + jnp.dot(qcat, st_ref[n], preferred_element_type=F32)
            mu = jnp.mean(y, axis=-1, keepdims=True)
            yc = y - mu
            var = jnp.mean(yc * yc, axis=-1, keepdims=True)
            yn = yc * lax.rsqrt(var + GN_EPS) * gnw + gnb
            o_ref[0, rows, :] = (yn * g_ref[0, rows, :].astype(F32)).astype(o_ref.dtype)
        return carry

    lax.fori_loop(0, nc // U, out_body, 0)


def _retention(ret, dec, gn_w, gn_b, batch, seq):
    blk = lambda off: pl.BlockSpec((1, seq, HEAD_DIM), lambda b, h: (off + h, b, 0))
    per_head = pl.BlockSpec((1, HEAD_DIM), lambda b, h: (0, h))
    return pl.pallas_call(
        functools.partial(_retention_kernel, seq=seq),
        grid=(batch, RET_HEADS),
        in_specs=[
            pl.BlockSpec((1, 8, HEAD_DIM), lambda b, h: (h, 0, 0)),
            blk(COL_RQ), blk(COL_RK), blk(COL_RV), blk(COL_RG),
            per_head, per_head,
        ],
        out_specs=pl.BlockSpec((1, seq, HEAD_DIM), lambda b, h: (h, b, 0)),
        out_shape=jax.ShapeDtypeStruct((RET_HEADS, batch * seq, HEAD_DIM), BF16),
        scratch_shapes=[
            pltpu.VMEM((seq // RET_CHUNK, 2 * HEAD_DIM, HEAD_DIM), F32),
            pltpu.VMEM((seq // RET_CHUNK, 2 * HEAD_DIM, HEAD_DIM), BF16),
        ],
        compiler_params=pltpu.CompilerParams(
            dimension_semantics=("arbitrary", "arbitrary"), vmem_limit_bytes=VMEM_LIMIT),
        name="retention",
    )(dec, ret, ret, ret, ret, gn_w.reshape(1, RET_WIDTH), gn_b.reshape(1, RET_WIDTH))


def _attention_kernel(sink_ref, q_ref, k_ref, v_ref, g0_ref, g1_ref, o_ref, vx_ref, *, seq):
    T = WINDOW
    i = pl.program_id(1)

    @pl.when(i == 0)
    def _():
        ones = jnp.ones((T, HEAD_DIM), BF16)

        def body(c, carry):
            rows = pl.ds(pl.multiple_of(c * T, T), T)
            for kh in range(ATT_KV_HEADS):
                vx_ref[rows, 2 * kh * HEAD_DIM:(2 * kh + 1) * HEAD_DIM] = v_ref[rows, kh * HEAD_DIM:(kh + 1) * HEAD_DIM]
                vx_ref[rows, (2 * kh + 1) * HEAD_DIM:(2 * kh + 2) * HEAD_DIM] = ones
            return carry

        lax.fori_loop(0, seq // T, body, 0)

    g_refs = (g0_ref, g1_ref)
    qrow = lax.broadcasted_iota(jnp.int32, (T, 3 * T), 0)
    kcol = lax.broadcasted_iota(jnp.int32, (T, 3 * T), 1)

    for t in range(ATT_TQ // T):
        rows = slice(t * T, (t + 1) * T)
        qstart = (i * (ATT_TQ // T) + t) * T
        kstart = pl.multiple_of(jnp.clip(qstart - T, 0, seq - 3 * T), T)
        bias = jnp.where(jnp.abs(qrow - kcol + (qstart - kstart)) <= WINDOW, 0.0, MASK_VALUE)
        for kh in range(ATT_KV_HEADS):
            q4 = jnp.concatenate(
                [q_ref[rows, (kh * ATT_GROUP + g) * HEAD_DIM:(kh * ATT_GROUP + g + 1) * HEAD_DIM]
                 for g in range(ATT_GROUP)], axis=0)
            kw = k_ref[pl.ds(kstart, 3 * T), kh * HEAD_DIM:(kh + 1) * HEAD_DIM]
            vw = vx_ref[pl.ds(kstart, 3 * T), 2 * kh * HEAD_DIM:(2 * kh + 2) * HEAD_DIM]
            s4 = _nt_dot(q4, kw)
            ps, ms, sinks = [], [], []
            for g in range(ATT_GROUP):
                s = s4[g * T:(g + 1) * T] + bias
                sink = sink_ref[kh * ATT_GROUP + g] * LOG2E
                m = jnp.maximum(jnp.max(s, axis=-1, keepdims=True), sink)
                ps.append(jnp.exp2(s - m).astype(BF16))
                ms.append(m)
                sinks.append(sink)
            pv = jnp.dot(jnp.concatenate(ps, axis=0), vw, preferred_element_type=F32)
            for g in range(ATT_GROUP):
                blk = pv[g * T:(g + 1) * T]
                denom = blk[:, HEAD_DIM:HEAD_DIM + 1] + jnp.exp2(sinks[g] - ms[g])
                gate = g_refs[kh][rows, g * HEAD_DIM:(g + 1) * HEAD_DIM].astype(F32)
                ocols = slice((kh * ATT_GROUP + g) * HEAD_DIM, (kh * ATT_GROUP + g + 1) * HEAD_DIM)
                o_ref[rows, ocols] = (blk[:, :HEAD_DIM] / denom * gate).astype(o_ref.dtype)


def _attention(aq, ak, av, ag, sink, batch, seq):
    nq = seq // ATT_TQ
    gw = ATT_GROUP * HEAD_DIM
    return pl.pallas_call(
        functools.partial(_attention_kernel, seq=seq),
        grid=(batch, nq),
        in_specs=[
            pl.BlockSpec(memory_space=pltpu.SMEM),
            pl.BlockSpec((ATT_TQ, ATT_WIDTH), lambda b, i: (b * nq + i, 0)),
            pl.BlockSpec((seq, ATT_KV_WIDTH), lambda b, i: (b, 0)),
            pl.BlockSpec((seq, ATT_KV_WIDTH), lambda b, i: (b, 0)),
            pl.BlockSpec((ATT_TQ, gw), lambda b, i: (b * nq + i, 0)),
            pl.BlockSpec((ATT_TQ, gw), lambda b, i: (b * nq + i, 1)),
        ],
        out_specs=pl.BlockSpec((ATT_TQ, ATT_WIDTH), lambda b, i: (b * nq + i, 0)),
        out_shape=jax.ShapeDtypeStruct((batch * seq, ATT_WIDTH), BF16),
        scratch_shapes=[pltpu.VMEM((seq, 2 * ATT_KV_WIDTH), BF16)],
        compiler_params=pltpu.CompilerParams(
            dimension_semantics=("arbitrary", "arbitrary"), vmem_limit_bytes=VMEM_LIMIT),
        name="attention",
    )(sink, aq, ak, av, ag, ag)


def _out_proj_kernel(x_ref, yr_ref, ya_ref, wr_ref, wa_ref, fw_ref, o_ref):
    yr = jnp.concatenate([yr_ref[h] for h in range(RET_HEADS)], axis=1)
    acc = jnp.dot(yr, wr_ref[...], preferred_element_type=F32)
    acc = acc + jnp.dot(ya_ref[...], wa_ref[...], preferred_element_type=F32)
    z = x_ref[...] + acc
    ms = jnp.mean(z * z, axis=-1, keepdims=True)
    o_ref[...] = z * lax.rsqrt(ms + NORM_EPS) * fw_ref[...]


def _out_proj(x2, y_ret, y_att, w_out_bf16, final_norm_w):
    m = x2.shape[0]
    return pl.pallas_call(
        _out_proj_kernel,
        grid=(m // OUT_TM,),
        in_specs=[
            pl.BlockSpec((OUT_TM, D_MODEL), lambda i: (i, 0)),
            pl.BlockSpec((RET_HEADS, OUT_TM, HEAD_DIM), lambda i: (0, i, 0)),
            pl.BlockSpec((OUT_TM, ATT_WIDTH), lambda i: (i, 0)),
            pl.BlockSpec((RET_WIDTH, D_MODEL), lambda i: (0, 0)),
            pl.BlockSpec((ATT_WIDTH, D_MODEL), lambda i: (RET_WIDTH // ATT_WIDTH, 0)),
            pl.BlockSpec((1, D_MODEL), lambda i: (0, 0)),
        ],
        out_specs=pl.BlockSpec((OUT_TM, D_MODEL), lambda i: (i, 0)),
        out_shape=jax.ShapeDtypeStruct((m, D_MODEL), F32),
        compiler_params=pltpu.CompilerParams(
            dimension_semantics=("arbitrary",), vmem_limit_bytes=VMEM_LIMIT),
        name="out_proj",
    )(x2, y_ret, y_att, w_out_bf16, w_out_bf16, final_norm_w.reshape(1, D_MODEL))


def _rotary_tables(seq):
    pos = jnp.arange(seq, dtype=F32)[:, None]
    half_r = HEAD_DIM // 2
    ang_r = pos * (RET_THETA ** (-jnp.arange(half_r, dtype=F32) / half_r))[None, :]
    cos_r, sin_r = jnp.cos(ang_r), jnp.sin(ang_r)
    half_a = ROPE_DIMS // 2
    ang_a = pos * (ROPE_THETA ** (-jnp.arange(half_a, dtype=F32) / half_a))[None, :]
    cos_a, sin_a = jnp.cos(ang_a), jnp.sin(ang_a)
    zeros = jnp.zeros((seq, half_a), F32)
    rest = jnp.zeros((seq, HEAD_DIM - ROPE_DIMS), F32)
    att_cos = jnp.concatenate([cos_a, cos_a, rest + 1.0], axis=1)
    att_lo = jnp.concatenate([-sin_a, zeros, rest], axis=1)
    att_hi = jnp.concatenate([zeros, sin_a, rest], axis=1)
    qscale = (HEAD_DIM ** -0.5) * LOG2E
    tabs = [None] * N_TABLES
    tabs[TAB_RET_COS] = jnp.concatenate([cos_r, cos_r], axis=1)
    tabs[TAB_RET_SIN] = jnp.concatenate([-sin_r, sin_r], axis=1)
    tabs[TAB_ATT_Q:TAB_ATT_Q + 3] = [att_cos * qscale, att_lo * qscale, att_hi * qscale]
    tabs[TAB_ATT_K:TAB_ATT_K + 3] = [att_cos, att_lo, att_hi]
    return jnp.stack(tabs)


def _layer(x2, w_in, w_out, norm_w, decay_fwd, decay_bwd, gn_w, gn_b, sink, post_norm_w, batch, seq):
    ret, aq, ak, av, ag = _in_proj(x2, norm_w, w_in.astype(BF16), _rotary_tables(seq), seq)
    dec = jnp.zeros((RET_HEADS, 8, HEAD_DIM), F32)
    dec = dec.at[:, 0, :].set(decay_fwd.astype(F32)[:, None])
    dec = dec.at[:, 1, :].set(decay_bwd.astype(F32)[:, None])
    y_ret = _retention(ret, dec, gn_w.astype(F32), gn_b.astype(F32), batch, seq)
    y_att = _attention(aq, ak, av, ag, sink.astype(F32), batch, seq)
    return _out_proj(x2, y_ret, y_att, w_out.astype(BF16), post_norm_w)


def kernel(x, w_in, w_out, norm_w, decay_fwd, decay_bwd, gn_w, gn_b, sink, final_norm_w):
    batch, seq, _ = x.shape
    depth = w_in.shape[0]
    assert depth == 1, "the final RMSNorm is fused into the single layer's output projection"
    assert seq % (RET_UNROLL * RET_CHUNK) == 0 and seq % ATT_TQ == 0 and seq >= 3 * WINDOW
    assert (batch * seq) % OUT_TM == 0 and N_COL_GROUPS % IN_CHUNK_GROUPS == 0
    x2 = x.reshape(batch * seq, D_MODEL)
    out = _layer(x2, w_in[0], w_out[0], norm_w[0], decay_fwd[0], decay_bwd[0], gn_w[0], gn_b[0], sink[0],
                 final_norm_w, batch, seq)
    return out.reshape(batch, seq, D_MODEL)
```

```python
import functools

import jax
import jax.numpy as jnp
from jax import lax
from jax.experimental import pallas as pl
from jax.experimental.pallas import tpu as pltpu

F32 = jnp.float32
BF16 = jnp.bfloat16

D_MODEL = 2048
HEAD_DIM = 128
RET_HEADS = 8
RET_WIDTH = RET_HEADS * HEAD_DIM
RET_CHUNK = 256
RET_THETA = 10000.0
GN_EPS = 1e-5
ATT_HEADS = 8
ATT_KV_HEADS = 2
ATT_GROUP = ATT_HEADS // ATT_KV_HEADS
ATT_WIDTH = ATT_HEADS * HEAD_DIM
ATT_KV_WIDTH = ATT_KV_HEADS * HEAD_DIM
WINDOW = 128
ROPE_THETA = 500000.0
ROPE_DIMS = HEAD_DIM // 4
NORM_EPS = 1e-6
MASK_VALUE = -1e30
LOG2E = 1.4426950408889634
D_IN_PROJ = 4 * RET_WIDTH + 2 * ATT_WIDTH + 2 * ATT_KV_WIDTH

COL_RQ = 0
COL_RK = COL_RQ + RET_HEADS
COL_RV = COL_RK + RET_HEADS
COL_RG = COL_RV + RET_HEADS
COL_AQ = COL_RG + RET_HEADS
COL_AK = COL_AQ + ATT_HEADS
COL_AV = COL_AK + ATT_KV_HEADS
COL_AG = COL_AV + ATT_KV_HEADS
N_COL_GROUPS = COL_AG + ATT_HEADS

PLAIN, RET_ROT, ATT_ROT_Q, ATT_ROT_K, SILU = range(5)
GROUP_KIND = ([RET_ROT] * (2 * RET_HEADS) + [PLAIN] * RET_HEADS + [SILU] * RET_HEADS
              + [ATT_ROT_Q] * ATT_HEADS + [ATT_ROT_K] * ATT_KV_HEADS + [PLAIN] * ATT_KV_HEADS + [SILU] * ATT_HEADS)
assert len(GROUP_KIND) == N_COL_GROUPS
TAB_RET_COS, TAB_RET_SIN, TAB_ATT_Q, TAB_ATT_K, N_TABLES = 0, 1, 2, 5, 8

IN_TM = 256
IN_CHUNK_GROUPS = 4
NORM_ROWS = 128
RET_UNROLL = 8
ATT_TQ = 512
OUT_TM = 1024
OUT_SUB = 256
OUT_CHUNK = 512
VMEM_LIMIT = 56 * 1024 * 1024


def _nt_dot(a, b):
    return lax.dot_general(a, b, (((1,), (1,)), ((), ())), preferred_element_type=F32)


def _tn_dot(a, b):
    return lax.dot_general(a, b, (((0,), (0,)), ((), ())), preferred_element_type=F32)


def _partial_rotary(x, cos, sin_lo, sin_hi):
    half = ROPE_DIMS // 2
    return (x * cos
            + pltpu.roll(x, HEAD_DIM - half, 1) * sin_lo
            + pltpu.roll(x, half, 1) * sin_hi)


def _in_proj_kernel(x_ref, nw_ref, w_ref, tab_ref, ret_ref, aq_ref, ak_ref, av_ref, ag_ref, h_ref):
    nw = nw_ref[...]

    def store(g, a):
        a = a.astype(BF16)
        if g < COL_AQ:
            ret_ref[g] = a
        else:
            for ref, lo, hi in ((aq_ref, COL_AQ, COL_AK), (ak_ref, COL_AK, COL_AV),
                                (av_ref, COL_AV, COL_AG), (ag_ref, COL_AG, N_COL_GROUPS)):
                if lo <= g < hi:
                    ref[:, (g - lo) * HEAD_DIM:(g - lo + 1) * HEAD_DIM] = a

    def norm_body(c, carry):
        rows = pl.ds(pl.multiple_of(c * NORM_ROWS, NORM_ROWS), NORM_ROWS)
        x = x_ref[rows, :]
        ms = jnp.mean(x * x, axis=-1, keepdims=True)
        h_ref[rows, :] = (x * lax.rsqrt(ms + NORM_EPS) * nw).astype(BF16)
        return carry

    lax.fori_loop(0, IN_TM // NORM_ROWS, norm_body, 0)

    h = h_ref[...]
    cw = IN_CHUNK_GROUPS * HEAD_DIM
    for c in range(N_COL_GROUPS // IN_CHUNK_GROUPS):
        acc = jnp.dot(h, w_ref[:, c * cw:(c + 1) * cw], preferred_element_type=F32)
        for u in range(IN_CHUNK_GROUPS):
            g = c * IN_CHUNK_GROUPS + u
            a = acc[:, u * HEAD_DIM:(u + 1) * HEAD_DIM]
            kind = GROUP_KIND[g]
            if kind == RET_ROT:
                a = a * tab_ref[TAB_RET_COS] + pltpu.roll(a, HEAD_DIM // 2, 1) * tab_ref[TAB_RET_SIN]
            elif kind == ATT_ROT_Q:
                a = _partial_rotary(a, *(tab_ref[TAB_ATT_Q + t] for t in range(3)))
            elif kind == ATT_ROT_K:
                a = _partial_rotary(a, *(tab_ref[TAB_ATT_K + t] for t in range(3)))
            elif kind == SILU:
                a = a * jax.nn.sigmoid(a)
            store(g, a)


def _in_proj(x2, norm_w, w_in_bf16, tables, seq):
    m = x2.shape[0]
    tiles_per_seq = seq // IN_TM
    assert tables.shape == (N_TABLES, seq, HEAD_DIM) and seq % IN_TM == 0 and m % seq == 0
    tok = lambda width: pl.BlockSpec((IN_TM, width), lambda i: (i, 0))
    tok_shape = lambda width: jax.ShapeDtypeStruct((m, width), BF16)
    return pl.pallas_call(
        _in_proj_kernel,
        grid=(m // IN_TM,),
        in_specs=[
            pl.BlockSpec((IN_TM, D_MODEL), lambda i: (i, 0)),
            pl.BlockSpec((1, D_MODEL), lambda i: (0, 0)),
            pl.BlockSpec((D_MODEL, D_IN_PROJ), lambda i: (0, 0), pipeline_mode=pl.Buffered(1)),
            pl.BlockSpec((N_TABLES, IN_TM, HEAD_DIM), lambda i: (0, i % tiles_per_seq, 0)),
        ],
        out_specs=[pl.BlockSpec((COL_AQ, IN_TM, HEAD_DIM), lambda i: (0, i, 0)),
                   tok(ATT_WIDTH), tok(ATT_KV_WIDTH), tok(ATT_KV_WIDTH), tok(ATT_WIDTH)],
        out_shape=[jax.ShapeDtypeStruct((COL_AQ, m, HEAD_DIM), BF16),
                   tok_shape(ATT_WIDTH), tok_shape(ATT_KV_WIDTH), tok_shape(ATT_KV_WIDTH), tok_shape(ATT_WIDTH)],
        scratch_shapes=[pltpu.VMEM((IN_TM, D_MODEL), BF16)],
        compiler_params=pltpu.CompilerParams(dimension_semantics=("arbitrary",), vmem_limit_bytes=VMEM_LIMIT),
        name="in_proj",
    )(x2, norm_w.reshape(1, D_MODEL), w_in_bf16, tables)


def _log_sigmoid(z):
    return jnp.minimum(z, 0.0) - jnp.log1p(jnp.exp(-jnp.abs(z)))


def _retention_kernel(dec_ref, q_ref, k_ref, v_ref, g_ref, gnw_ref, gnb_ref, o_ref, kv_ref, st_ref, *, seq):
    C = RET_CHUNK
    nc = seq // C
    U = RET_UNROLL
    scale = HEAD_DIM ** -0.5

    lg_f = _log_sigmoid(dec_ref[0, 0:1, :])
    lg_b = _log_sigmoid(dec_ref[0, 1:2, :])
    diff = (lax.broadcasted_iota(jnp.int32, (C, C), 0) - lax.broadcasted_iota(jnp.int32, (C, C), 1)).astype(F32)
    dmat = scale * jnp.where(diff >= 0.0,
                             jnp.exp(lg_f * jnp.maximum(diff, 0.0)),
                             jnp.exp(lg_b * jnp.maximum(-diff, 0.0)))
    lg_f, lg_b = lg_f[:, :HEAD_DIM], lg_b[:, :HEAD_DIM]
    row = lax.broadcasted_iota(jnp.int32, (C, HEAD_DIM), 0).astype(F32)
    kdec_f = (scale * jnp.exp(lg_f * (C - 1.0 - row))).astype(BF16)
    kdec_b = (scale * jnp.exp(lg_b * row)).astype(BF16)
    qdec_f = jnp.exp(lg_f * (row + 1.0)).astype(BF16)
    qdec_b = jnp.exp(lg_b * (C - row)).astype(BF16)
    cdec_f = jnp.exp(lg_f * float(C))
    cdec_b = jnp.exp(lg_b * float(C))

    def chunk_rows(n):
        return pl.ds(pl.multiple_of(n * C, C), C)

    def kv_body(it, carry):
        for u in range(U):
            n = it * U + u
            rows = chunk_rows(n)
            k = k_ref[0, rows, :]
            kd = jnp.concatenate([k * kdec_f, k * kdec_b], axis=1)
            kv_ref[n] = _tn_dot(kd, v_ref[0, rows, :])
        return carry

    lax.fori_loop(0, nc // U, kv_body, 0)

    def scan_body(t, carry):
        sf, sb = carry
        m = nc - 1 - t
        st_ref[t, 0:HEAD_DIM, :] = sf.astype(BF16)
        st_ref[m, HEAD_DIM:2 * HEAD_DIM, :] = sb.astype(BF16)
        sf = sf * cdec_f + kv_ref[t, 0:HEAD_DIM, :]
        sb = sb * cdec_b + kv_ref[m, HEAD_DIM:2 * HEAD_DIM, :]
        return sf, sb

    zero = jnp.zeros((HEAD_DIM, HEAD_DIM), F32)
    lax.fori_loop(0, nc, scan_body, (zero, zero))

    gnw = gnw_ref[...]
    gnb = gnb_ref[...]

    def out_body(it, carry):
        for u in range(U):
            n = it * U + u
            rows = chunk_rows(n)
            q = q_ref[0, rows, :]
            vn = v_ref[0, rows, :]
            scores = _nt_dot(q, k_ref[0, rows, :]) * dmat
            inner = jnp.dot(scores.astype(BF16), vn, preferred_element_type=F32)
            qcat = jnp.concatenate([q * qdec_f, q * qdec_b], axis=1)
            y = inner + jnp.dot(qcat, st_ref[n], preferred_element_type=F32)
            mu = jnp.mean(y, axis=-1, keepdims=True)
            yc = y - mu
            var = jnp.mean(yc * yc, axis=-1, keepdims=True)
            yn = yc * lax.rsqrt(var + GN_EPS) * gnw + gnb
            o_ref[0, rows, :] = (yn * g_ref[0, rows, :].astype(F32)).astype(o_ref.dtype)
        return carry

    lax.fori_loop(0, nc // U, out_body, 0)


def _retention(ret, dec, gn_w, gn_b, batch, seq):
    blk = lambda off: pl.BlockSpec((1, seq, HEAD_DIM), lambda b, h: (off + h, b, 0))
    per_head = pl.BlockSpec((1, HEAD_DIM), lambda b, h: (0, h))
    return pl.pallas_call(
        functools.partial(_retention_kernel, seq=seq),
        grid=(batch, RET_HEADS),
        in_specs=[
            pl.BlockSpec((1, 8, RET_CHUNK), lambda b, h: (h, 0, 0)),
            blk(COL_RQ), blk(COL_RK), blk(COL_RV), blk(COL_RG),
            per_head, per_head,
        ],
        out_specs=pl.BlockSpec((1, seq, HEAD_DIM), lambda b, h: (h, b, 0)),
        out_shape=jax.ShapeDtypeStruct((RET_HEADS, batch * seq, HEAD_DIM), BF16),
        scratch_shapes=[
            pltpu.VMEM((seq // RET_CHUNK, 2 * HEAD_DIM, HEAD_DIM), F32),
            pltpu.VMEM((seq // RET_CHUNK, 2 * HEAD_DIM, HEAD_DIM), BF16),
        ],
        compiler_params=pltpu.CompilerParams(
            dimension_semantics=("arbitrary", "arbitrary"), vmem_limit_bytes=VMEM_LIMIT),
        name="retention",
    )(dec, ret, ret, ret, ret, gn_w.reshape(1, RET_WIDTH), gn_b.reshape(1, RET_WIDTH))


def _attention_kernel(sink_ref, q_ref, k_ref, v_ref, g0_ref, g1_ref, o_ref, vx_ref, *, seq):
    T = WINDOW
    i = pl.program_id(1)

    @pl.when(i == 0)
    def _():
        ones = jnp.ones((T, HEAD_DIM), BF16)

        def body(c, carry):
            rows = pl.ds(pl.multiple_of(c * T, T), T)
            for kh in range(ATT_KV_HEADS):
                vx_ref[rows, 2 * kh * HEAD_DIM:(2 * kh + 1) * HEAD_DIM] = v_ref[rows, kh * HEAD_DIM:(kh + 1) * HEAD_DIM]
                vx_ref[rows, (2 * kh + 1) * HEAD_DIM:(2 * kh + 2) * HEAD_DIM] = ones
            return carry

        lax.fori_loop(0, seq // T, body, 0)

    g_refs = (g0_ref, g1_ref)
    qrow = lax.broadcasted_iota(jnp.int32, (T, 3 * T), 0)
    kcol = lax.broadcasted_iota(jnp.int32, (T, 3 * T), 1)

    for t in range(ATT_TQ // T):
        rows = slice(t * T, (t + 1) * T)
        qstart = (i * (ATT_TQ // T) + t) * T
        kstart = pl.multiple_of(jnp.clip(qstart - T, 0, seq - 3 * T), T)
        bias = jnp.where(jnp.abs(qrow - kcol + (qstart - kstart)) <= WINDOW, 0.0, MASK_VALUE)
        for kh in range(ATT_KV_HEADS):
            q4 = jnp.concatenate(
                [q_ref[rows, (kh * ATT_GROUP + g) * HEAD_DIM:(kh * ATT_GROUP + g + 1) * HEAD_DIM]
                 for g in range(ATT_GROUP)], axis=0)
            kw = k_ref[pl.ds(kstart, 3 * T), kh * HEAD_DIM:(kh + 1) * HEAD_DIM]
            vw = vx_ref[pl.ds(kstart, 3 * T), 2 * kh * HEAD_DIM:(2 * kh + 2) * HEAD_DIM]
            ps, ms, sinks = [], [], []
            for g in range(ATT_GROUP):
                s = _nt_dot(q4[g * T:(g + 1) * T], kw) + bias
                sink = sink_ref[kh * ATT_GROUP + g] * LOG2E
                m = jnp.maximum(jnp.max(s, axis=-1, keepdims=True), sink)
                ps.append(jnp.exp2(s - m).astype(BF16))
                ms.append(m)
                sinks.append(sink)
            pv = jnp.dot(jnp.concatenate(ps, axis=0), vw, preferred_element_type=F32)
            for g in range(ATT_GROUP):
                blk = pv[g * T:(g + 1) * T]
                denom = blk[:, HEAD_DIM:HEAD_DIM + 1] + jnp.exp2(sinks[g] - ms[g])
                gate = g_refs[kh][rows, g * HEAD_DIM:(g + 1) * HEAD_DIM].astype(F32)
                ocols = slice((kh * ATT_GROUP + g) * HEAD_DIM, (kh * ATT_GROUP + g + 1) * HEAD_DIM)
                o_ref[rows, ocols] = (blk[:, :HEAD_DIM] / denom * gate).astype(o_ref.dtype)


def _attention(aq, ak, av, ag, sink, batch, seq):
    nq = seq // ATT_TQ
    gw = ATT_GROUP * HEAD_DIM
    return pl.pallas_call(
        functools.partial(_attention_kernel, seq=seq),
        grid=(batch, nq),
        in_specs=[
            pl.BlockSpec(memory_space=pltpu.SMEM),
            pl.BlockSpec((ATT_TQ, ATT_WIDTH), lambda b, i: (b * nq + i, 0)),
            pl.BlockSpec((seq, ATT_KV_WIDTH), lambda b, i: (b, 0)),
            pl.BlockSpec((seq, ATT_KV_WIDTH), lambda b, i: (b, 0)),
            pl.BlockSpec((ATT_TQ, gw), lambda b, i: (b * nq + i, 0)),
            pl.BlockSpec((ATT_TQ, gw), lambda b, i: (b * nq + i, 1)),
        ],
        out_specs=pl.BlockSpec((ATT_TQ, ATT_WIDTH), lambda b, i: (b * nq + i, 0)),
        out_shape=jax.ShapeDtypeStruct((batch * seq, ATT_WIDTH), BF16),
        scratch_shapes=[pltpu.VMEM((seq, 2 * ATT_KV_WIDTH), BF16)],
        compiler_params=pltpu.CompilerParams(
            dimension_semantics=("arbitrary", "arbitrary"), vmem_limit_bytes=VMEM_LIMIT),
        name="attention",
    )(sink, aq, ak, av, ag, ag)


def _out_proj_kernel(x_ref, yr_ref, ya_ref, w_ref, fw_ref, o_ref):
    for r in range(OUT_TM // OUT_SUB):
        rows = slice(r * OUT_SUB, (r + 1) * OUT_SUB)
        y = jnp.concatenate([yr_ref[h, rows, :] for h in range(RET_HEADS)] + [ya_ref[rows, :]], axis=1)
        ssq = jnp.zeros((OUT_SUB, HEAD_DIM), F32)
        for c in range(D_MODEL // OUT_CHUNK):
            cols = slice(c * OUT_CHUNK, (c + 1) * OUT_CHUNK)
            z = x_ref[rows, cols] + jnp.dot(y, w_ref[:, cols], preferred_element_type=F32)
            o_ref[rows, cols] = z
            for u in range(OUT_CHUNK // HEAD_DIM):
                zu = z[:, u * HEAD_DIM:(u + 1) * HEAD_DIM]
                ssq = ssq + zu * zu
        rs = lax.rsqrt(jnp.sum(ssq, axis=-1, keepdims=True) * (1.0 / D_MODEL) + NORM_EPS)
        for c in range(D_MODEL // OUT_CHUNK):
            cols = slice(c * OUT_CHUNK, (c + 1) * OUT_CHUNK)
            o_ref[rows, cols] = o_ref[rows, cols] * rs * fw_ref[:, cols]


def _out_proj(x2, y_ret, y_att, w_out_bf16, final_norm_w):
    m = x2.shape[0]
    return pl.pallas_call(
        _out_proj_kernel,
        grid=(m // OUT_TM,),
        in_specs=[
            pl.BlockSpec((OUT_TM, D_MODEL), lambda i: (i, 0)),
            pl.BlockSpec((RET_HEADS, OUT_TM, HEAD_DIM), lambda i: (0, i, 0)),
            pl.BlockSpec((OUT_TM, ATT_WIDTH), lambda i: (i, 0)),
            pl.BlockSpec((RET_WIDTH + ATT_WIDTH, D_MODEL), lambda i: (0, 0), pipeline_mode=pl.Buffered(1)),
            pl.BlockSpec((1, D_MODEL), lambda i: (0, 0)),
        ],
        out_specs=pl.BlockSpec((OUT_TM, D_MODEL), lambda i: (i, 0)),
        out_shape=jax.ShapeDtypeStruct((m, D_MODEL), F32),
        compiler_params=pltpu.CompilerParams(
            dimension_semantics=("arbitrary",), vmem_limit_bytes=VMEM_LIMIT),
        name="out_proj",
    )(x2, y_ret, y_att, w_out_bf16, final_norm_w.reshape(1, D_MODEL))


def _rotary_tables(seq):
    pos = jnp.arange(seq, dtype=F32)[:, None]
    half_r = HEAD_DIM // 2
    ang_r = pos * (RET_THETA ** (-jnp.arange(half_r, dtype=F32) / half_r))[None, :]
    cos_r, sin_r = jnp.cos(ang_r), jnp.sin(ang_r)
    half_a = ROPE_DIMS // 2
    ang_a = pos * (ROPE_THETA ** (-jnp.arange(half_a, dtype=F32) / half_a))[None, :]
    cos_a, sin_a = jnp.cos(ang_a), jnp.sin(ang_a)
    zeros = jnp.zeros((seq, half_a), F32)
    rest = jnp.zeros((seq, HEAD_DIM - ROPE_DIMS), F32)
    att_cos = jnp.concatenate([cos_a, cos_a, rest + 1.0], axis=1)
    att_lo = jnp.concatenate([-sin_a, zeros, rest], axis=1)
    att_hi = jnp.concatenate([zeros, sin_a, rest], axis=1)
    qscale = (HEAD_DIM ** -0.5) * LOG2E
    tabs = [None] * N_TABLES
    tabs[TAB_RET_COS] = jnp.concatenate([cos_r, cos_r], axis=1)
    tabs[TAB_RET_SIN] = jnp.concatenate([-sin_r, sin_r], axis=1)
    tabs[TAB_ATT_Q:TAB_ATT_Q + 3] = [att_cos * qscale, att_lo * qscale, att_hi * qscale]
    tabs[TAB_ATT_K:TAB_ATT_K + 3] = [att_cos, att_lo, att_hi]
    return jnp.stack(tabs)


def _layer(x2, w_in, w_out, norm_w, decay_fwd, decay_bwd, gn_w, gn_b, sink, post_norm_w, batch, seq):
    ret, aq, ak, av, ag = _in_proj(x2, norm_w, w_in.astype(BF16), _rotary_tables(seq), seq)
    dec = jnp.zeros((RET_HEADS, 8, RET_CHUNK), F32)
    dec = dec.at[:, 0, :].set(decay_fwd.astype(F32)[:, None])
    dec = dec.at[:, 1, :].set(decay_bwd.astype(F32)[:, None])
    y_ret = _retention(ret, dec, gn_w.astype(F32), gn_b.astype(F32), batch, seq)
    y_att = _attention(aq, ak, av, ag, sink.astype(F32), batch, seq)
    return _out_proj(x2, y_ret, y_att, w_out.astype(BF16), post_norm_w)


def kernel(x, w_in, w_out, norm_w, decay_fwd, decay_bwd, gn_w, gn_b, sink, final_norm_w):
    batch, seq, _ = x.shape
    depth = w_in.shape[0]
    assert depth == 1, "the final RMSNorm is fused into the single layer's output projection"
    assert seq % (RET_UNROLL * RET_CHUNK) == 0 and seq % ATT_TQ == 0 and seq >= 3 * WINDOW
    assert (batch * seq) % OUT_TM == 0 and N_COL_GROUPS % IN_CHUNK_GROUPS == 0
    x2 = x.reshape(batch * seq, D_MODEL)
    out = _layer(x2, w_in[0], w_out[0], norm_w[0], decay_fwd[0], decay_bwd[0], gn_w[0], gn_b[0], sink[0],
                 final_norm_w, batch, seq)
    return out.reshape(batch, seq, D_MODEL)
```

```python
import functools

import jax
import jax.numpy as jnp
from jax import lax
from jax.experimental import pallas as pl
from jax.experimental.pallas import tpu as pltpu

F32 = jnp.float32
BF16 = jnp.bfloat16

D_MODEL = 2048
HEAD_DIM = 128
RET_HEADS = 8
RET_WIDTH = RET_HEADS * HEAD_DIM
RET_CHUNK = 256
RET_THETA = 10000.0
GN_EPS = 1e-5
ATT_HEADS = 8
ATT_KV_HEADS = 2
ATT_GROUP = ATT_HEADS // ATT_KV_HEADS
ATT_WIDTH = ATT_HEADS * HEAD_DIM
ATT_KV_WIDTH = ATT_KV_HEADS * HEAD_DIM
WINDOW = 128
ROPE_THETA = 500000.0
ROPE_DIMS = HEAD_DIM // 4
NORM_EPS = 1e-6
MASK_VALUE = -1e30
LOG2E = 1.4426950408889634
D_IN_PROJ = 4 * RET_WIDTH + 2 * ATT_WIDTH + 2 * ATT_KV_WIDTH

COL_RQ = 0
COL_RK = COL_RQ + RET_HEADS
COL_RV = COL_RK + RET_HEADS
COL_RG = COL_RV + RET_HEADS
COL_AQ = COL_RG + RET_HEADS
COL_AK = COL_AQ + ATT_HEADS
COL_AV = COL_AK + ATT_KV_HEADS
COL_AG = COL_AV + ATT_KV_HEADS
N_COL_GROUPS = COL_AG + ATT_HEADS

PLAIN, RET_ROT, ATT_ROT_Q, ATT_ROT_K, SILU = range(5)
GROUP_KIND = ([RET_ROT] * (2 * RET_HEADS) + [PLAIN] * RET_HEADS + [SILU] * RET_HEADS
              + [ATT_ROT_Q] * ATT_HEADS + [ATT_ROT_K] * ATT_KV_HEADS + [PLAIN] * ATT_KV_HEADS + [SILU] * ATT_HEADS)
assert len(GROUP_KIND) == N_COL_GROUPS
TAB_RET_COS, TAB_RET_SIN, TAB_ATT_COS, TAB_ATT_SIN, N_TABLES = 0, 1, 2, 3, 4
ATT_Q_SCALE = (HEAD_DIM ** -0.5) * LOG2E
TRIG_SPLIT = 64

IN_TM = 256
IN_CHUNK_GROUPS = 4
NORM_ROWS = 128
RET_UNROLL = 8
ATT_TQ = 512
OUT_TM = 1024
OUT_SUB = 256
OUT_CHUNK = 512
VMEM_LIMIT = 56 * 1024 * 1024


def _nt_dot(a, b):
    return lax.dot_general(a, b, (((1,), (1,)), ((), ())), preferred_element_type=F32)


def _tn_dot(a, b):
    return lax.dot_general(a, b, (((0,), (0,)), ((), ())), preferred_element_type=F32)


def _partial_rotary(x, cos, sin_signed):
    half = ROPE_DIMS // 2
    lane = lax.broadcasted_iota(jnp.int32, x.shape, 1)
    partner = jnp.where(lane < half,
                        pltpu.roll(x, HEAD_DIM - half, 1),
                        pltpu.roll(x, half, 1))
    return x * cos + partner * sin_signed


def _in_proj_kernel(x0_ref, xn_ref, nw_ref, w_ref, tab_ref, ret_ref, aq_ref, ak_ref, av_ref, ag_ref, ha_ref, hb_ref):
    i = pl.program_id(0)
    nw = nw_ref[...]

    def normalise(x_ref, h_ref):
        for c in range(IN_TM // NORM_ROWS):
            rows = slice(c * NORM_ROWS, (c + 1) * NORM_ROWS)
            x = x_ref[rows, :]
            ms = jnp.mean(x * x, axis=-1, keepdims=True)
            h_ref[rows, :] = (x * lax.rsqrt(ms + NORM_EPS) * nw).astype(BF16)

    def store(g, a):
        a = a.astype(BF16)
        if g < COL_AQ:
            ret_ref[g] = a
        else:
            for ref, lo, hi in ((aq_ref, COL_AQ, COL_AK), (ak_ref, COL_AK, COL_AV),
                                (av_ref, COL_AV, COL_AG), (ag_ref, COL_AG, N_COL_GROUPS)):
                if lo <= g < hi:
                    ref[:, (g - lo) * HEAD_DIM:(g - lo + 1) * HEAD_DIM] = a

    def step(h_cur_ref, h_next_ref):
        normalise(xn_ref, h_next_ref)
        h = h_cur_ref[...]
        cw = IN_CHUNK_GROUPS * HEAD_DIM
        for c in range(N_COL_GROUPS // IN_CHUNK_GROUPS):
            acc = jnp.dot(h, w_ref[:, c * cw:(c + 1) * cw], preferred_element_type=F32)
            for u in range(IN_CHUNK_GROUPS):
                g = c * IN_CHUNK_GROUPS + u
                a = acc[:, u * HEAD_DIM:(u + 1) * HEAD_DIM]
                kind = GROUP_KIND[g]
                if kind == RET_ROT:
                    a = a * tab_ref[TAB_RET_COS] + pltpu.roll(a, HEAD_DIM // 2, 1) * tab_ref[TAB_RET_SIN]
                elif kind == ATT_ROT_Q:
                    a = _partial_rotary(a, tab_ref[TAB_ATT_COS], tab_ref[TAB_ATT_SIN]) * ATT_Q_SCALE
                elif kind == ATT_ROT_K:
                    a = _partial_rotary(a, tab_ref[TAB_ATT_COS], tab_ref[TAB_ATT_SIN])
                elif kind == SILU:
                    a = a * jax.nn.sigmoid(a)
                store(g, a)

    @pl.when(i == 0)
    def _():
        normalise(x0_ref, ha_ref)

    @pl.when(i % 2 == 0)
    def _():
        step(ha_ref, hb_ref)

    @pl.when(i % 2 == 1)
    def _():
        step(hb_ref, ha_ref)


def _in_proj(x2, norm_w, w_in_bf16, tables, seq):
    m = x2.shape[0]
    tiles_per_seq = seq // IN_TM
    assert tables.shape == (N_TABLES, seq, HEAD_DIM) and seq % IN_TM == 0 and m % seq == 0
    tok = lambda width: pl.BlockSpec((IN_TM, width), lambda i: (i, 0))
    tok_shape = lambda width: jax.ShapeDtypeStruct((m, width), BF16)
    return pl.pallas_call(
        _in_proj_kernel,
        grid=(m // IN_TM,),
        in_specs=[
            pl.BlockSpec((IN_TM, D_MODEL), lambda i: (0, 0)),
            pl.BlockSpec((IN_TM, D_MODEL), lambda i: (jnp.minimum(i + 1, m // IN_TM - 1), 0)),
            pl.BlockSpec((1, D_MODEL), lambda i: (0, 0)),
            pl.BlockSpec((D_MODEL, D_IN_PROJ), lambda i: (0, 0), pipeline_mode=pl.Buffered(1)),
            pl.BlockSpec((N_TABLES, IN_TM, HEAD_DIM), lambda i: (0, i % tiles_per_seq, 0)),
        ],
        out_specs=[pl.BlockSpec((COL_AQ, IN_TM, HEAD_DIM), lambda i: (0, i, 0)),
                   tok(ATT_WIDTH), tok(ATT_KV_WIDTH), tok(ATT_KV_WIDTH), tok(ATT_WIDTH)],
        out_shape=[jax.ShapeDtypeStruct((COL_AQ, m, HEAD_DIM), BF16),
                   tok_shape(ATT_WIDTH), tok_shape(ATT_KV_WIDTH), tok_shape(ATT_KV_WIDTH), tok_shape(ATT_WIDTH)],
        scratch_shapes=[pltpu.VMEM((IN_TM, D_MODEL), BF16), pltpu.VMEM((IN_TM, D_MODEL), BF16)],
        compiler_params=pltpu.CompilerParams(dimension_semantics=("arbitrary",), vmem_limit_bytes=VMEM_LIMIT),
        name="in_proj",
    )(x2, x2, norm_w.reshape(1, D_MODEL), w_in_bf16, tables)


def _log_sigmoid(z):
    return jnp.minimum(z, 0.0) - jnp.log1p(jnp.exp(-jnp.abs(z)))


def _retention_kernel(dec_ref, q_ref, k_ref, v_ref, g_ref, gnw_ref, gnb_ref, o_ref, kv_ref, st_ref, *, seq):
    C = RET_CHUNK
    nc = seq // C
    U = RET_UNROLL
    scale = HEAD_DIM ** -0.5

    lg_f = _log_sigmoid(dec_ref[0, 0:1, :])
    lg_b = _log_sigmoid(dec_ref[0, 1:2, :])
    diff = (lax.broadcasted_iota(jnp.int32, (C, C), 0) - lax.broadcasted_iota(jnp.int32, (C, C), 1)).astype(F32)
    dmat = scale * jnp.where(diff >= 0.0,
                             jnp.exp(lg_f * jnp.maximum(diff, 0.0)),
                             jnp.exp(lg_b * jnp.maximum(-diff, 0.0)))
    lg_f, lg_b = lg_f[:, :HEAD_DIM], lg_b[:, :HEAD_DIM]
    row = lax.broadcasted_iota(jnp.int32, (C, HEAD_DIM), 0).astype(F32)
    kdec_f = (scale * jnp.exp(lg_f * (C - 1.0 - row))).astype(BF16)
    kdec_b = (scale * jnp.exp(lg_b * row)).astype(BF16)
    qdec_f = jnp.exp(lg_f * (row + 1.0)).astype(BF16)
    qdec_b = jnp.exp(lg_b * (C - row)).astype(BF16)
    cdec_f = jnp.exp(lg_f * float(C))
    cdec_b = jnp.exp(lg_b * float(C))

    def chunk_rows(n):
        return pl.ds(pl.multiple_of(n * C, C), C)

    def kv_body(it, carry):
        for u in range(U):
            n = it * U + u
            rows = chunk_rows(n)
            k = k_ref[0, rows, :]
            kd = jnp.concatenate([k * kdec_f, k * kdec_b], axis=1)
            kv_ref[n] = _tn_dot(kd, v_ref[0, rows, :])
        return carry

    lax.fori_loop(0, nc // U, kv_body, 0)

    def scan_body(t, carry):
        sf, sb = carry
        m = nc - 1 - t
        st_ref[t, 0:HEAD_DIM, :] = sf.astype(BF16)
        st_ref[m, HEAD_DIM:2 * HEAD_DIM, :] = sb.astype(BF16)
        sf = sf * cdec_f + kv_ref[t, 0:HEAD_DIM, :]
        sb = sb * cdec_b + kv_ref[m, HEAD_DIM:2 * HEAD_DIM, :]
        return sf, sb

    zero = jnp.zeros((HEAD_DIM, HEAD_DIM), F32)
    lax.fori_loop(0, nc, scan_body, (zero, zero))

    gnw = gnw_ref[...]
    gnb = gnb_ref[...]

    def out_body(it, carry):
        for u in range(U):
            n = it * U + u
            rows = chunk_rows(n)
            q = q_ref[0, rows, :]
            vn = v_ref[0, rows, :]
            scores = _nt_dot(q, k_ref[0, rows, :]) * dmat
            inner = jnp.dot(scores.astype(BF16), vn, preferred_element_type=F32)
            qcat = jnp.concatenate([q * qdec_f, q * qdec_b], axis=1)
            y = inner + jnp.dot(qcat, st_ref[n], preferred_element_type=F32)
            mu = jnp.mean(y, axis=-1, keepdims=True)
            yc = y - mu
            var = jnp.mean(yc * yc, axis=-1, keepdims=True)
            yn = yc * lax.rsqrt(var + GN_EPS) * gnw + gnb
            o_ref[0, rows, :] = (yn * g_ref[0, rows, :].astype(F32)).astype(o_ref.dtype)
        return carry

    lax.fori_loop(0, nc // U, out_body, 0)


def _retention(ret, dec, gn_w, gn_b, batch, seq):
    blk = lambda off: pl.BlockSpec((1, seq, HEAD_DIM), lambda b, h: (off + h, b, 0))
    per_head = pl.BlockSpec((1, HEAD_DIM), lambda b, h: (0, h))
    return pl.pallas_call(
        functools.partial(_retention_kernel, seq=seq),
        grid=(batch, RET_HEADS),
        in_specs=[
            pl.BlockSpec((1, 8, RET_CHUNK), lambda b, h: (h, 0, 0)),
            blk(COL_RQ), blk(COL_RK), blk(COL_RV), blk(COL_RG),
            per_head, per_head,
        ],
        out_specs=pl.BlockSpec((1, seq, HEAD_DIM), lambda b, h: (h, b, 0)),
        out_shape=jax.ShapeDtypeStruct((RET_HEADS, batch * seq, HEAD_DIM), BF16),
        scratch_shapes=[
            pltpu.VMEM((seq // RET_CHUNK, 2 * HEAD_DIM, HEAD_DIM), F32),
            pltpu.VMEM((seq // RET_CHUNK, 2 * HEAD_DIM, HEAD_DIM), BF16),
        ],
        compiler_params=pltpu.CompilerParams(
            dimension_semantics=("arbitrary", "arbitrary"), vmem_limit_bytes=VMEM_LIMIT),
        name="retention",
    )(dec, ret, ret, ret, ret, gn_w.reshape(1, RET_WIDTH), gn_b.reshape(1, RET_WIDTH))


def _attention_kernel(sink_ref, q_ref, k_ref, v_ref, g0_ref, g1_ref, o_ref, vx_ref, *, seq):
    T = WINDOW
    i = pl.program_id(1)

    @pl.when(i == 0)
    def _():
        ones = jnp.ones((T, HEAD_DIM), BF16)

        def body(c, carry):
            rows = pl.ds(pl.multiple_of(c * T, T), T)
            for kh in range(ATT_KV_HEADS):
                vx_ref[rows, 2 * kh * HEAD_DIM:(2 * kh + 1) * HEAD_DIM] = v_ref[rows, kh * HEAD_DIM:(kh + 1) * HEAD_DIM]
                vx_ref[rows, (2 * kh + 1) * HEAD_DIM:(2 * kh + 2) * HEAD_DIM] = ones
            return carry

        lax.fori_loop(0, seq // T, body, 0)

    g_refs = (g0_ref, g1_ref)
    qrow = lax.broadcasted_iota(jnp.int32, (T, 3 * T), 0)
    kcol = lax.broadcasted_iota(jnp.int32, (T, 3 * T), 1)

    for t in range(ATT_TQ // T):
        rows = slice(t * T, (t + 1) * T)
        qstart = (i * (ATT_TQ // T) + t) * T
        kstart = pl.multiple_of(jnp.clip(qstart - T, 0, seq - 3 * T), T)
        bias = jnp.where(jnp.abs(qrow - kcol + (qstart - kstart)) <= WINDOW, 0.0, MASK_VALUE)
        for kh in range(ATT_KV_HEADS):
            q4 = jnp.concatenate(
                [q_ref[rows, (kh * ATT_GROUP + g) * HEAD_DIM:(kh * ATT_GROUP + g + 1) * HEAD_DIM]
                 for g in range(ATT_GROUP)], axis=0)
            kw = k_ref[pl.ds(kstart, 3 * T), kh * HEAD_DIM:(kh + 1) * HEAD_DIM]
            vw = vx_ref[pl.ds(kstart, 3 * T), 2 * kh * HEAD_DIM:(2 * kh + 2) * HEAD_DIM]
            ps, ms, sinks = [], [], []
            for g in range(ATT_GROUP):
                s = _nt_dot(q4[g * T:(g + 1) * T], kw) + bias
                sink = sink_ref[kh * ATT_GROUP + g] * LOG2E
                m = jnp.maximum(jnp.max(s, axis=-1, keepdims=True), sink)
                ps.append(jnp.exp2(s - m).astype(BF16))
                ms.append(m)
                sinks.append(sink)
            pv = jnp.dot(jnp.concatenate(ps, axis=0), vw, preferred_element_type=F32)
            for g in range(ATT_GROUP):
                blk = pv[g * T:(g + 1) * T]
                denom = blk[:, HEAD_DIM:HEAD_DIM + 1] + jnp.exp2(sinks[g] - ms[g])
                gate = g_refs[kh][rows, g * HEAD_DIM:(g + 1) * HEAD_DIM].astype(F32)
                ocols = slice((kh * ATT_GROUP + g) * HEAD_DIM, (kh * ATT_GROUP + g + 1) * HEAD_DIM)
                o_ref[rows, ocols] = (blk[:, :HEAD_DIM] / denom * gate).astype(o_ref.dtype)


def _attention(aq, ak, av, ag, sink, batch, seq):
    nq = seq // ATT_TQ
    gw = ATT_GROUP * HEAD_DIM
    return pl.pallas_call(
        functools.partial(_attention_kernel, seq=seq),
        grid=(batch, nq),
        in_specs=[
            pl.BlockSpec(memory_space=pltpu.SMEM),
            pl.BlockSpec((ATT_TQ, ATT_WIDTH), lambda b, i: (b * nq + i, 0)),
            pl.BlockSpec((seq, ATT_KV_WIDTH), lambda b, i: (b, 0)),
            pl.BlockSpec((seq, ATT_KV_WIDTH), lambda b, i: (b, 0)),
            pl.BlockSpec((ATT_TQ, gw), lambda b, i: (b * nq + i, 0)),
            pl.BlockSpec((ATT_TQ, gw), lambda b, i: (b * nq + i, 1)),
        ],
        out_specs=pl.BlockSpec((ATT_TQ, ATT_WIDTH), lambda b, i: (b * nq + i, 0)),
        out_shape=jax.ShapeDtypeStruct((batch * seq, ATT_WIDTH), BF16),
        scratch_shapes=[pltpu.VMEM((seq, 2 * ATT_KV_WIDTH), BF16)],
        compiler_params=pltpu.CompilerParams(
            dimension_semantics=("arbitrary", "arbitrary"), vmem_limit_bytes=VMEM_LIMIT),
        name="attention",
    )(sink, aq, ak, av, ag, ag)


def _out_proj_kernel(x_ref, yr_ref, ya_ref, w_ref, fw_ref, o_ref):
    for r in range(OUT_TM // OUT_SUB):
        rows = slice(r * OUT_SUB, (r + 1) * OUT_SUB)
        y = jnp.concatenate([yr_ref[h, rows, :] for h in range(RET_HEADS)] + [ya_ref[rows, :]], axis=1)
        ssq = jnp.zeros((OUT_SUB, HEAD_DIM), F32)
        for c in range(D_MODEL // OUT_CHUNK):
            cols = slice(c * OUT_CHUNK, (c + 1) * OUT_CHUNK)
            z = x_ref[rows, cols] + jnp.dot(y, w_ref[:, cols], preferred_element_type=F32)
            o_ref[rows, cols] = z
            for u in range(OUT_CHUNK // HEAD_DIM):
                zu = z[:, u * HEAD_DIM:(u + 1) * HEAD_DIM]
                ssq = ssq + zu * zu
        rs = lax.rsqrt(jnp.sum(ssq, axis=-1, keepdims=True) * (1.0 / D_MODEL) + NORM_EPS)
        for c in range(D_MODEL // OUT_CHUNK):
            cols = slice(c * OUT_CHUNK, (c + 1) * OUT_CHUNK)
            o_ref[rows, cols] = o_ref[rows, cols] * rs * fw_ref[:, cols]


def _out_proj(x2, y_ret, y_att, w_out_bf16, final_norm_w):
    m = x2.shape[0]
    return pl.pallas_call(
        _out_proj_kernel,
        grid=(m // OUT_TM,),
        in_specs=[
            pl.BlockSpec((OUT_TM, D_MODEL), lambda i: (i, 0)),
            pl.BlockSpec((RET_HEADS, OUT_TM, HEAD_DIM), lambda i: (0, i, 0)),
            pl.BlockSpec((OUT_TM, ATT_WIDTH), lambda i: (i, 0)),
            pl.BlockSpec((RET_WIDTH + ATT_WIDTH, D_MODEL), lambda i: (0, 0), pipeline_mode=pl.Buffered(1)),
            pl.BlockSpec((1, D_MODEL), lambda i: (0, 0)),
        ],
        out_specs=pl.BlockSpec((OUT_TM, D_MODEL), lambda i: (i, 0)),
        out_shape=jax.ShapeDtypeStruct((m, D_MODEL), F32),
        compiler_params=pltpu.CompilerParams(
            dimension_semantics=("arbitrary",), vmem_limit_bytes=VMEM_LIMIT),
        name="out_proj",
    )(x2, y_ret, y_att, w_out_bf16, final_norm_w.reshape(1, D_MODEL))


def _rotary_tables(seq):
    def cos_sin(theta, half):
        inv_freq = (theta ** (-jnp.arange(half, dtype=F32) / half))[None, :]
        hi = jnp.arange(seq // TRIG_SPLIT, dtype=F32)[:, None] * TRIG_SPLIT * inv_freq
        lo = jnp.arange(TRIG_SPLIT, dtype=F32)[:, None] * inv_freq
        ch, sh, cl, sl = jnp.cos(hi)[:, None], jnp.sin(hi)[:, None], jnp.cos(lo)[None], jnp.sin(lo)[None]
        return (ch * cl - sh * sl).reshape(seq, half), (sh * cl + ch * sl).reshape(seq, half)

    cos_r, sin_r = cos_sin(RET_THETA, HEAD_DIM // 2)
    cos_a, sin_a = cos_sin(ROPE_THETA, ROPE_DIMS // 2)
    rest = jnp.zeros((seq, HEAD_DIM - ROPE_DIMS), F32)
    tabs = [None] * N_TABLES
    tabs[TAB_RET_COS] = jnp.concatenate([cos_r, cos_r], axis=1)
    tabs[TAB_RET_SIN] = jnp.concatenate([-sin_r, sin_r], axis=1)
    tabs[TAB_ATT_COS] = jnp.concatenate([cos_a, cos_a, rest + 1.0], axis=1)
    tabs[TAB_ATT_SIN] = jnp.concatenate([-sin_a, sin_a, rest], axis=1)
    return jnp.stack(tabs)


def _layer(x2, w_in, w_out, norm_w, decay_fwd, decay_bwd, gn_w, gn_b, sink, post_norm_w, batch, seq):
    ret, aq, ak, av, ag = _in_proj(x2, norm_w, w_in.astype(BF16), _rotary_tables(seq), seq)
    dec = jnp.zeros((RET_HEADS, 8, RET_CHUNK), F32)
    dec = dec.at[:, 0, :].set(decay_fwd.astype(F32)[:, None])
    dec = dec.at[:, 1, :].set(decay_bwd.astype(F32)[:, None])
    y_ret = _retention(ret, dec, gn_w.astype(F32), gn_b.astype(F32), batch, seq)
    y_att = _attention(aq, ak, av, ag, sink.astype(F32), batch, seq)
    return _out_proj(x2, y_ret, y_att, w_out.astype(BF16), post_norm_w)


def kernel(x, w_in, w_out, norm_w, decay_fwd, decay_bwd, gn_w, gn_b, sink, final_norm_w):
    batch, seq, _ = x.shape
    depth = w_in.shape[0]
    assert depth == 1, "the final RMSNorm is fused into the single layer's output projection"
    assert seq % (RET_UNROLL * RET_CHUNK) == 0 and seq % ATT_TQ == 0 and seq >= 3 * WINDOW
    assert (batch * seq) % OUT_TM == 0 and N_COL_GROUPS % IN_CHUNK_GROUPS == 0 and seq % TRIG_SPLIT == 0
    x2 = x.reshape(batch * seq, D_MODEL)
    out = _layer(x2, w_in[0], w_out[0], norm_w[0], decay_fwd[0], decay_bwd[0], gn_w[0], gn_b[0], sink[0],
                 final_norm_w, batch, seq)
    return out.reshape(batch, seq, D_MODEL)
```

```python
import functools

import jax
import jax.numpy as jnp
from jax import lax
from jax.experimental import pallas as pl
from jax.experimental.pallas import tpu as pltpu

F32 = jnp.float32
BF16 = jnp.bfloat16

D_MODEL = 2048
HEAD_DIM = 128
RET_HEADS = 8
RET_WIDTH = RET_HEADS * HEAD_DIM
RET_CHUNK = 256
RET_THETA = 10000.0
GN_EPS = 1e-5
ATT_HEADS = 8
ATT_KV_HEADS = 2
ATT_GROUP = ATT_HEADS // ATT_KV_HEADS
ATT_WIDTH = ATT_HEADS * HEAD_DIM
ATT_KV_WIDTH = ATT_KV_HEADS * HEAD_DIM
WINDOW = 128
ROPE_THETA = 500000.0
ROPE_DIMS = HEAD_DIM // 4
NORM_EPS = 1e-6
MASK_VALUE = -1e30
LOG2E = 1.4426950408889634
D_IN_PROJ = 4 * RET_WIDTH + 2 * ATT_WIDTH + 2 * ATT_KV_WIDTH

COL_RQ = 0
COL_RK = COL_RQ + RET_HEADS
COL_RV = COL_RK + RET_HEADS
COL_RG = COL_RV + RET_HEADS
COL_AQ = COL_RG + RET_HEADS
COL_AK = COL_AQ + ATT_HEADS
COL_AV = COL_AK + ATT_KV_HEADS
COL_AG = COL_AV + ATT_KV_HEADS
N_COL_GROUPS = COL_AG + ATT_HEADS

PLAIN, RET_ROT, ATT_ROT_Q, ATT_ROT_K, SILU = range(5)
GROUP_KIND = ([RET_ROT] * (2 * RET_HEADS) + [PLAIN] * RET_HEADS + [SILU] * RET_HEADS
              + [ATT_ROT_Q] * ATT_HEADS + [ATT_ROT_K] * ATT_KV_HEADS + [PLAIN] * ATT_KV_HEADS + [SILU] * ATT_HEADS)
assert len(GROUP_KIND) == N_COL_GROUPS
TAB_RET_COS, TAB_RET_SIN, TAB_ATT_COS, TAB_ATT_SIN, N_TABLES = 0, 1, 2, 3, 4
ATT_Q_SCALE = (HEAD_DIM ** -0.5) * LOG2E
TRIG_SPLIT = 64

IN_TM = 256
IN_CHUNK_GROUPS = 4
NORM_ROWS = 128
IN_STAGE_COLS = 256
RET_UNROLL = 8
ATT_TQ = 512
OUT_TM = 512
OUT_STAGE_COLS = 256
OUT_SUB = 256
OUT_CHUNK = 512
VMEM_LIMIT = 56 * 1024 * 1024


def _nt_dot(a, b):
    return lax.dot_general(a, b, (((1,), (1,)), ((), ())), preferred_element_type=F32)


def _tn_dot(a, b):
    return lax.dot_general(a, b, (((0,), (0,)), ((), ())), preferred_element_type=F32)


def _partial_rotary(x, cos, sin_signed):
    half = ROPE_DIMS // 2
    lane = lax.broadcasted_iota(jnp.int32, x.shape, 1)
    partner = jnp.where(lane < half,
                        pltpu.roll(x, HEAD_DIM - half, 1),
                        pltpu.roll(x, half, 1))
    return x * cos + partner * sin_signed


def _in_proj_kernel(x0_ref, xn_ref, nw_ref, w_hbm_ref, tab_ref, ret_ref, aq_ref, ak_ref, av_ref, ag_ref,
                    w_ref, stage_ref, sem_ref, ha_ref, hb_ref):
    i = pl.program_id(0)
    nw = nw_ref[...]
    cw = IN_CHUNK_GROUPS * HEAD_DIM
    n_chunks = N_COL_GROUPS // IN_CHUNK_GROUPS
    n_stage = D_IN_PROJ // IN_STAGE_COLS
    stage_per_chunk = cw // IN_STAGE_COLS

    def stage_copy(j):
        slot = j % 2
        return pltpu.make_async_copy(w_hbm_ref.at[:, pl.ds(j * IN_STAGE_COLS, IN_STAGE_COLS)],
                                     stage_ref.at[slot], sem_ref.at[slot])

    def normalise(x_ref, h_ref):
        for c in range(IN_TM // NORM_ROWS):
            rows = slice(c * NORM_ROWS, (c + 1) * NORM_ROWS)
            x = x_ref[rows, :]
            ms = jnp.mean(x * x, axis=-1, keepdims=True)
            h_ref[rows, :] = (x * lax.rsqrt(ms + NORM_EPS) * nw).astype(BF16)

    def store(g, a):
        a = a.astype(BF16)
        if g < COL_AQ:
            ret_ref[g] = a
        else:
            for ref, lo, hi in ((aq_ref, COL_AQ, COL_AK), (ak_ref, COL_AK, COL_AV),
                                (av_ref, COL_AV, COL_AG), (ag_ref, COL_AG, N_COL_GROUPS)):
                if lo <= g < hi:
                    ref[:, (g - lo) * HEAD_DIM:(g - lo + 1) * HEAD_DIM] = a

    def step(h_cur_ref, h_next_ref, stream_weights):
        normalise(xn_ref, h_next_ref)
        h = h_cur_ref[...]
        for c in range(n_chunks):
            if stream_weights:
                for j in range(c * stage_per_chunk, (c + 1) * stage_per_chunk):
                    stage_copy(j).wait()
                    w_ref[:, j * IN_STAGE_COLS:(j + 1) * IN_STAGE_COLS] = stage_ref[j % 2].astype(BF16)
                    if j + 2 < n_stage:
                        stage_copy(j + 2).start()
            acc = jnp.dot(h, w_ref[:, c * cw:(c + 1) * cw], preferred_element_type=F32)
            for u in range(IN_CHUNK_GROUPS):
                g = c * IN_CHUNK_GROUPS + u
                a = acc[:, u * HEAD_DIM:(u + 1) * HEAD_DIM]
                kind = GROUP_KIND[g]
                if kind == RET_ROT:
                    a = a * tab_ref[TAB_RET_COS] + pltpu.roll(a, HEAD_DIM // 2, 1) * tab_ref[TAB_RET_SIN]
                elif kind == ATT_ROT_Q:
                    a = _partial_rotary(a, tab_ref[TAB_ATT_COS], tab_ref[TAB_ATT_SIN]) * ATT_Q_SCALE
                elif kind == ATT_ROT_K:
                    a = _partial_rotary(a, tab_ref[TAB_ATT_COS], tab_ref[TAB_ATT_SIN])
                elif kind == SILU:
                    a = a * jax.nn.sigmoid(a)
                store(g, a)

    @pl.when(i == 0)
    def _():
        stage_copy(0).start()
        stage_copy(1).start()
        normalise(x0_ref, ha_ref)
        step(ha_ref, hb_ref, True)

    @pl.when((i > 0) & (i % 2 == 0))
    def _():
        step(ha_ref, hb_ref, False)

    @pl.when(i % 2 == 1)
    def _():
        step(hb_ref, ha_ref, False)


def _in_proj(x2, norm_w, w_in, tables, seq):
    m = x2.shape[0]
    tiles_per_seq = seq // IN_TM
    assert tables.shape == (N_TABLES, seq, HEAD_DIM) and seq % IN_TM == 0 and m % seq == 0
    tok = lambda width: pl.BlockSpec((IN_TM, width), lambda i: (i, 0))
    tok_shape = lambda width: jax.ShapeDtypeStruct((m, width), BF16)
    return pl.pallas_call(
        _in_proj_kernel,
        grid=(m // IN_TM,),
        in_specs=[
            pl.BlockSpec((IN_TM, D_MODEL), lambda i: (0, 0), pipeline_mode=pl.Buffered(1)),
            pl.BlockSpec((IN_TM, D_MODEL), lambda i: (jnp.minimum(i + 1, m // IN_TM - 1), 0)),
            pl.BlockSpec((1, D_MODEL), lambda i: (0, 0)),
            pl.BlockSpec(memory_space=pl.ANY),
            pl.BlockSpec((N_TABLES, IN_TM, HEAD_DIM), lambda i: (0, i % tiles_per_seq, 0)),
        ],
        out_specs=[pl.BlockSpec((COL_AQ, IN_TM, HEAD_DIM), lambda i: (0, i, 0)),
                   tok(ATT_WIDTH), tok(ATT_KV_WIDTH), tok(ATT_KV_WIDTH), tok(ATT_WIDTH)],
        out_shape=[jax.ShapeDtypeStruct((COL_AQ, m, HEAD_DIM), BF16),
                   tok_shape(ATT_WIDTH), tok_shape(ATT_KV_WIDTH), tok_shape(ATT_KV_WIDTH), tok_shape(ATT_WIDTH)],
        scratch_shapes=[
            pltpu.VMEM((D_MODEL, D_IN_PROJ), BF16),
            pltpu.VMEM((2, D_MODEL, IN_STAGE_COLS), F32),
            pltpu.SemaphoreType.DMA((2,)),
            pltpu.VMEM((IN_TM, D_MODEL), BF16), pltpu.VMEM((IN_TM, D_MODEL), BF16),
        ],
        compiler_params=pltpu.CompilerParams(dimension_semantics=("arbitrary",), vmem_limit_bytes=VMEM_LIMIT),
        name="in_proj",
    )(x2, x2, norm_w.reshape(1, D_MODEL), w_in, tables)


def _log_sigmoid(z):
    return jnp.minimum(z, 0.0) - jnp.log1p(jnp.exp(-jnp.abs(z)))


def _retention_kernel(dec_ref, q_ref, k_ref, v_ref, g_ref, gnw_ref, gnb_ref, o_ref, kv_ref, st_ref, *, seq):
    C = RET_CHUNK
    nc = seq // C
    U = RET_UNROLL
    scale = HEAD_DIM ** -0.5

    lg_f = _log_sigmoid(dec_ref[0, 0:1, :])
    lg_b = _log_sigmoid(dec_ref[0, 1:2, :])
    diff = (lax.broadcasted_iota(jnp.int32, (C, C), 0) - lax.broadcasted_iota(jnp.int32, (C, C), 1)).astype(F32)
    dmat = scale * jnp.where(diff >= 0.0,
                             jnp.exp(lg_f * jnp.maximum(diff, 0.0)),
                             jnp.exp(lg_b * jnp.maximum(-diff, 0.0)))
    lg_f, lg_b = lg_f[:, :HEAD_DIM], lg_b[:, :HEAD_DIM]
    row = lax.broadcasted_iota(jnp.int32, (C, HEAD_DIM), 0).astype(F32)
    kdec_f = (scale * jnp.exp(lg_f * (C - 1.0 - row))).astype(BF16)
    kdec_b = (scale * jnp.exp(lg_b * row)).astype(BF16)
    qdec_f = jnp.exp(lg_f * (row + 1.0)).astype(BF16)
    qdec_b = jnp.exp(lg_b * (C - row)).astype(BF16)
    cdec_f = jnp.exp(lg_f * float(C))
    cdec_b = jnp.exp(lg_b * float(C))

    def chunk_rows(n):
        return pl.ds(pl.multiple_of(n * C, C), C)

    def kv_body(it, carry):
        for u in range(U):
            n = it * U + u
            rows = chunk_rows(n)
            k = k_ref[0, rows, :]
            kd = jnp.concatenate([k * kdec_f, k * kdec_b], axis=1)
            kv_ref[n] = _tn_dot(kd, v_ref[0, rows, :])
        return carry

    lax.fori_loop(0, nc // U, kv_body, 0)

    def scan_body(t, carry):
        sf, sb = carry
        m = nc - 1 - t
        st_ref[t, 0:HEAD_DIM, :] = sf.astype(BF16)
        st_ref[m, HEAD_DIM:2 * HEAD_DIM, :] = sb.astype(BF16)
        sf = sf * cdec_f + kv_ref[t, 0:HEAD_DIM, :]
        sb = sb * cdec_b + kv_ref[m, HEAD_DIM:2 * HEAD_DIM, :]
        return sf, sb

    zero = jnp.zeros((HEAD_DIM, HEAD_DIM), F32)
    lax.fori_loop(0, nc, scan_body, (zero, zero))

    gnw = gnw_ref[...]
    gnb = gnb_ref[...]

    def out_body(it, carry):
        for u in range(U):
            n = it * U + u
            rows = chunk_rows(n)
            q = q_ref[0, rows, :]
            vn = v_ref[0, rows, :]
            scores = _nt_dot(q, k_ref[0, rows, :]) * dmat
            inner = jnp.dot(scores.astype(BF16), vn, preferred_element_type=F32)
            qcat = jnp.concatenate([q * qdec_f, q * qdec_b], axis=1)
            y = inner + jnp.dot(qcat, st_ref[n], preferred_element_type=F32)
            mu = jnp.mean(y, axis=-1, keepdims=True)
            yc = y - mu
            var = jnp.mean(yc * yc, axis=-1, keepdims=True)
            yn = yc * lax.rsqrt(var + GN_EPS) * gnw + gnb
            o_ref[0, rows, :] = (yn * g_ref[0, rows, :].astype(F32)).astype(o_ref.dtype)
        return carry

    lax.fori_loop(0, nc // U, out_body, 0)


def _retention(ret, dec, gn_w, gn_b, batch, seq):
    blk = lambda off: pl.BlockSpec((1, seq, HEAD_DIM), lambda b, h: (off + h, b, 0))
    per_head = pl.BlockSpec((1, HEAD_DIM), lambda b, h: (0, h))
    return pl.pallas_call(
        functools.partial(_retention_kernel, seq=seq),
        grid=(batch, RET_HEADS),
        in_specs=[
            pl.BlockSpec((1, 8, RET_CHUNK), lambda b, h: (h, 0, 0)),
            blk(COL_RQ), blk(COL_RK), blk(COL_RV), blk(COL_RG),
            per_head, per_head,
        ],
        out_specs=pl.BlockSpec((1, seq, HEAD_DIM), lambda b, h: (h, b, 0)),
        out_shape=jax.ShapeDtypeStruct((RET_HEADS, batch * seq, HEAD_DIM), BF16),
        scratch_shapes=[
            pltpu.VMEM((seq // RET_CHUNK, 2 * HEAD_DIM, HEAD_DIM), F32),
            pltpu.VMEM((seq // RET_CHUNK, 2 * HEAD_DIM, HEAD_DIM), BF16),
        ],
        compiler_params=pltpu.CompilerParams(
            dimension_semantics=("arbitrary", "arbitrary"), vmem_limit_bytes=VMEM_LIMIT),
        name="retention",
    )(dec, ret, ret, ret, ret, gn_w.reshape(1, RET_WIDTH), gn_b.reshape(1, RET_WIDTH))


def _attention_kernel(sink_ref, q_ref, k_ref, v_ref, g0_ref, g1_ref, o_ref, vx_ref, *, seq):
    T = WINDOW
    i = pl.program_id(1)

    @pl.when(i == 0)
    def _():
        ones = jnp.ones((T, HEAD_DIM), BF16)

        def body(c, carry):
            rows = pl.ds(pl.multiple_of(c * T, T), T)
            for kh in range(ATT_KV_HEADS):
                vx_ref[rows, 2 * kh * HEAD_DIM:(2 * kh + 1) * HEAD_DIM] = v_ref[rows, kh * HEAD_DIM:(kh + 1) * HEAD_DIM]
                vx_ref[rows, (2 * kh + 1) * HEAD_DIM:(2 * kh + 2) * HEAD_DIM] = ones
            return carry

        lax.fori_loop(0, seq // T, body, 0)

    g_refs = (g0_ref, g1_ref)
    qrow = lax.broadcasted_iota(jnp.int32, (T, 3 * T), 0)
    kcol = lax.broadcasted_iota(jnp.int32, (T, 3 * T), 1)

    for t in range(ATT_TQ // T):
        rows = slice(t * T, (t + 1) * T)
        qstart = (i * (ATT_TQ // T) + t) * T
        kstart = pl.multiple_of(jnp.clip(qstart - T, 0, seq - 3 * T), T)
        bias = jnp.where(jnp.abs(qrow - kcol + (qstart - kstart)) <= WINDOW, 0.0, MASK_VALUE)
        for kh in range(ATT_KV_HEADS):
            q4 = jnp.concatenate(
                [q_ref[rows, (kh * ATT_GROUP + g) * HEAD_DIM:(kh * ATT_GROUP + g + 1) * HEAD_DIM]
                 for g in range(ATT_GROUP)], axis=0)
            kw = k_ref[pl.ds(kstart, 3 * T), kh * HEAD_DIM:(kh + 1) * HEAD_DIM]
            vw = vx_ref[pl.ds(kstart, 3 * T), 2 * kh * HEAD_DIM:(2 * kh + 2) * HEAD_DIM]
            ps, ms, sinks = [], [], []
            for g in range(ATT_GROUP):
                s = _nt_dot(q4[g * T:(g + 1) * T], kw) + bias
                sink = sink_ref[kh * ATT_GROUP + g] * LOG2E
                m = jnp.maximum(jnp.max(s, axis=-1, keepdims=True), sink)
                ps.append(jnp.exp2(s - m).astype(BF16))
                ms.append(m)
                sinks.append(sink)
            pv = jnp.dot(jnp.concatenate(ps, axis=0), vw, preferred_element_type=F32)
            for g in range(ATT_GROUP):
                blk = pv[g * T:(g + 1) * T]
                denom = blk[:, HEAD_DIM:HEAD_DIM + 1] + jnp.exp2(sinks[g] - ms[g])
                gate = g_refs[kh][rows, g * HEAD_DIM:(g + 1) * HEAD_DIM].astype(F32)
                ocols = slice((kh * ATT_GROUP + g) * HEAD_DIM, (kh * ATT_GROUP + g + 1) * HEAD_DIM)
                o_ref[rows, ocols] = (blk[:, :HEAD_DIM] / denom * gate).astype(o_ref.dtype)


def _attention(aq, ak, av, ag, sink, batch, seq):
    nq = seq // ATT_TQ
    gw = ATT_GROUP * HEAD_DIM
    return pl.pallas_call(
        functools.partial(_attention_kernel, seq=seq),
        grid=(batch, nq),
        in_specs=[
            pl.BlockSpec(memory_space=pltpu.SMEM),
            pl.BlockSpec((ATT_TQ, ATT_WIDTH), lambda b, i: (b * nq + i, 0)),
            pl.BlockSpec((seq, ATT_KV_WIDTH), lambda b, i: (b, 0)),
            pl.BlockSpec((seq, ATT_KV_WIDTH), lambda b, i: (b, 0)),
            pl.BlockSpec((ATT_TQ, gw), lambda b, i: (b * nq + i, 0)),
            pl.BlockSpec((ATT_TQ, gw), lambda b, i: (b * nq + i, 1)),
        ],
        out_specs=pl.BlockSpec((ATT_TQ, ATT_WIDTH), lambda b, i: (b * nq + i, 0)),
        out_shape=jax.ShapeDtypeStruct((batch * seq, ATT_WIDTH), BF16),
        scratch_shapes=[pltpu.VMEM((seq, 2 * ATT_KV_WIDTH), BF16)],
        compiler_params=pltpu.CompilerParams(
            dimension_semantics=("arbitrary", "arbitrary"), vmem_limit_bytes=VMEM_LIMIT),
        name="attention",
    )(sink, aq, ak, av, ag, ag)


def _out_proj_kernel(x_ref, yr_ref, ya_ref, w_hbm_ref, fw_ref, o_ref, w_ref, stage_ref, sem_ref):
    n_stage = D_MODEL // OUT_STAGE_COLS
    stage_per_chunk = OUT_CHUNK // OUT_STAGE_COLS

    def stage_copy(j):
        slot = j % 2
        return pltpu.make_async_copy(w_hbm_ref.at[:, pl.ds(j * OUT_STAGE_COLS, OUT_STAGE_COLS)],
                                     stage_ref.at[slot], sem_ref.at[slot])

    def tile(stream_weights):
        for r in range(OUT_TM // OUT_SUB):
            rows = slice(r * OUT_SUB, (r + 1) * OUT_SUB)
            y = jnp.concatenate([yr_ref[h, rows, :] for h in range(RET_HEADS)] + [ya_ref[rows, :]], axis=1)
            ssq = jnp.zeros((OUT_SUB, HEAD_DIM), F32)
            for c in range(D_MODEL // OUT_CHUNK):
                if stream_weights and r == 0:
                    for j in range(c * stage_per_chunk, (c + 1) * stage_per_chunk):
                        stage_copy(j).wait()
                        w_ref[:, j * OUT_STAGE_COLS:(j + 1) * OUT_STAGE_COLS] = stage_ref[j % 2].astype(BF16)
                        if j + 2 < n_stage:
                            stage_copy(j + 2).start()
                cols = slice(c * OUT_CHUNK, (c + 1) * OUT_CHUNK)
                z = x_ref[rows, cols] + jnp.dot(y, w_ref[:, cols], preferred_element_type=F32)
                o_ref[rows, cols] = z
                for u in range(OUT_CHUNK // HEAD_DIM):
                    zu = z[:, u * HEAD_DIM:(u + 1) * HEAD_DIM]
                    ssq = ssq + zu * zu
            rs = lax.rsqrt(jnp.sum(ssq, axis=-1, keepdims=True) * (1.0 / D_MODEL) + NORM_EPS)
            for c in range(D_MODEL // OUT_CHUNK):
                cols = slice(c * OUT_CHUNK, (c + 1) * OUT_CHUNK)
                o_ref[rows, cols] = o_ref[rows, cols] * rs * fw_ref[:, cols]

    @pl.when(pl.program_id(0) == 0)
    def _():
        stage_copy(0).start()
        stage_copy(1).start()
        tile(True)

    @pl.when(pl.program_id(0) > 0)
    def _():
        tile(False)


def _out_proj(x2, y_ret, y_att, w_out, final_norm_w):
    m = x2.shape[0]
    return pl.pallas_call(
        _out_proj_kernel,
        grid=(m // OUT_TM,),
        in_specs=[
            pl.BlockSpec((OUT_TM, D_MODEL), lambda i: (i, 0)),
            pl.BlockSpec((RET_HEADS, OUT_TM, HEAD_DIM), lambda i: (0, i, 0)),
            pl.BlockSpec((OUT_TM, ATT_WIDTH), lambda i: (i, 0)),
            pl.BlockSpec(memory_space=pl.ANY),
            pl.BlockSpec((1, D_MODEL), lambda i: (0, 0)),
        ],
        out_specs=pl.BlockSpec((OUT_TM, D_MODEL), lambda i: (i, 0)),
        out_shape=jax.ShapeDtypeStruct((m, D_MODEL), F32),
        scratch_shapes=[
            pltpu.VMEM((RET_WIDTH + ATT_WIDTH, D_MODEL), BF16),
            pltpu.VMEM((2, RET_WIDTH + ATT_WIDTH, OUT_STAGE_COLS), F32),
            pltpu.SemaphoreType.DMA((2,)),
        ],
        compiler_params=pltpu.CompilerParams(
            dimension_semantics=("arbitrary",), vmem_limit_bytes=VMEM_LIMIT),
        name="out_proj",
    )(x2, y_ret, y_att, w_out, final_norm_w.reshape(1, D_MODEL))


def _rotary_tables(seq):
    def cos_sin(theta, half):
        inv_freq = (theta ** (-jnp.arange(half, dtype=F32) / half))[None, :]
        hi = jnp.arange(seq // TRIG_SPLIT, dtype=F32)[:, None] * TRIG_SPLIT * inv_freq
        lo = jnp.arange(TRIG_SPLIT, dtype=F32)[:, None] * inv_freq
        ch, sh, cl, sl = jnp.cos(hi)[:, None], jnp.sin(hi)[:, None], jnp.cos(lo)[None], jnp.sin(lo)[None]
        return (ch * cl - sh * sl).reshape(seq, half), (sh * cl + ch * sl).reshape(seq, half)

    cos_r, sin_r = cos_sin(RET_THETA, HEAD_DIM // 2)
    cos_a, sin_a = cos_sin(ROPE_THETA, ROPE_DIMS // 2)
    rest = jnp.zeros((seq, HEAD_DIM - ROPE_DIMS), F32)
    tabs = [None] * N_TABLES
    tabs[TAB_RET_COS] = jnp.concatenate([cos_r, cos_r], axis=1)
    tabs[TAB_RET_SIN] = jnp.concatenate([-sin_r, sin_r], axis=1)
    tabs[TAB_ATT_COS] = jnp.concatenate([cos_a, cos_a, rest + 1.0], axis=1)
    tabs[TAB_ATT_SIN] = jnp.concatenate([-sin_a, sin_a, rest], axis=1)
    return jnp.stack(tabs)


def _layer(x2, w_in, w_out, norm_w, decay_fwd, decay_bwd, gn_w, gn_b, sink, post_norm_w, batch, seq):
    ret, aq, ak, av, ag = _in_proj(x2, norm_w, w_in.astype(F32), _rotary_tables(seq), seq)
    dec = jnp.zeros((RET_HEADS, 8, RET_CHUNK), F32)
    dec = dec.at[:, 0, :].set(decay_fwd.astype(F32)[:, None])
    dec = dec.at[:, 1, :].set(decay_bwd.astype(F32)[:, None])
    y_ret = _retention(ret, dec, gn_w.astype(F32), gn_b.astype(F32), batch, seq)
    y_att = _attention(aq, ak, av, ag, sink.astype(F32), batch, seq)
    return _out_proj(x2, y_ret, y_att, w_out.astype(F32), post_norm_w)


def kernel(x, w_in, w_out, norm_w, decay_fwd, decay_bwd, gn_w, gn_b, sink, final_norm_w):
    batch, seq, _ = x.shape
    depth = w_in.shape[0]
    assert depth == 1, "the final RMSNorm is fused into the single layer's output projection"
    assert seq % (RET_UNROLL * RET_CHUNK) == 0 and seq % ATT_TQ == 0 and seq >= 3 * WINDOW
    assert (batch * seq) % OUT_TM == 0 and N_COL_GROUPS % IN_CHUNK_GROUPS == 0 and seq % TRIG_SPLIT == 0
    x2 = x.reshape(batch * seq, D_MODEL)
    out = _layer(x2, w_in[0], w_out[0], norm_w[0], decay_fwd[0], decay_bwd[0], gn_w[0], gn_b[0], sink[0],
                 final_norm_w, batch, seq)
    return out.reshape(batch, seq, D_MODEL)
```

```python
import functools

import jax
import jax.numpy as jnp
from jax import lax
from jax.experimental import pallas as pl
from jax.experimental.pallas import tpu as pltpu

F32 = jnp.float32
BF16 = jnp.bfloat16

D_MODEL = 2048
HEAD_DIM = 128
RET_HEADS = 8
RET_WIDTH = RET_HEADS * HEAD_DIM
RET_CHUNK = 256
RET_THETA = 10000.0
GN_EPS = 1e-5
ATT_HEADS = 8
ATT_KV_HEADS = 2
ATT_GROUP = ATT_HEADS // ATT_KV_HEADS
ATT_WIDTH = ATT_HEADS * HEAD_DIM
ATT_KV_WIDTH = ATT_KV_HEADS * HEAD_DIM
WINDOW = 128
ROPE_THETA = 500000.0
ROPE_DIMS = HEAD_DIM // 4
NORM_EPS = 1e-6
MASK_VALUE = -1e30
LOG2E = 1.4426950408889634
D_IN_PROJ = 4 * RET_WIDTH + 2 * ATT_WIDTH + 2 * ATT_KV_WIDTH

COL_RQ = 0
COL_RK = COL_RQ + RET_HEADS
COL_RV = COL_RK + RET_HEADS
COL_RG = COL_RV + RET_HEADS
COL_AQ = COL_RG + RET_HEADS
COL_AK = COL_AQ + ATT_HEADS
COL_AV = COL_AK + ATT_KV_HEADS
COL_AG = COL_AV + ATT_KV_HEADS
N_COL_GROUPS = COL_AG + ATT_HEADS

PLAIN, RET_ROT, ATT_ROT_Q, ATT_ROT_K, SILU = range(5)
GROUP_KIND = ([RET_ROT] * (2 * RET_HEADS) + [PLAIN] * RET_HEADS + [SILU] * RET_HEADS
              + [ATT_ROT_Q] * ATT_HEADS + [ATT_ROT_K] * ATT_KV_HEADS + [PLAIN] * ATT_KV_HEADS + [SILU] * ATT_HEADS)
assert len(GROUP_KIND) == N_COL_GROUPS
TAB_RET_COS, TAB_RET_SIN, TAB_ATT_COS, TAB_ATT_SIN, N_TABLES = 0, 1, 2, 3, 4
ATT_Q_SCALE = (HEAD_DIM ** -0.5) * LOG2E
TRIG_SPLIT = 64

IN_TM = 256
IN_CHUNK_GROUPS = 4
NORM_ROWS = 128
IN_STAGE_COLS = 256
STAGE_SLOTS = 4
RET_UNROLL = 16
ATT_TQ = 512
OUT_TM = 512
OUT_STAGE_COLS = 256
OUT_SUB = 256
OUT_CHUNK = 512
VMEM_LIMIT = 56 * 1024 * 1024


def _nt_dot(a, b):
    return lax.dot_general(a, b, (((1,), (1,)), ((), ())), preferred_element_type=F32)


def _tn_dot(a, b):
    return lax.dot_general(a, b, (((0,), (0,)), ((), ())), preferred_element_type=F32)


def _partial_rotary(x, cos, sin_signed):
    half = ROPE_DIMS // 2
    lane = lax.broadcasted_iota(jnp.int32, x.shape, 1)
    partner = jnp.where(lane < half,
                        pltpu.roll(x, HEAD_DIM - half, 1),
                        pltpu.roll(x, half, 1))
    return x * cos + partner * sin_signed


def _in_proj_kernel(x0_ref, xn_ref, nw_ref, w_hbm_ref, tab_ref, ret_ref, aq_ref, ak_ref, av_ref, ag_ref,
                    w_ref, stage_ref, sem_ref, ha_ref, hb_ref):
    i = pl.program_id(0)
    nw = nw_ref[...]
    cw = IN_CHUNK_GROUPS * HEAD_DIM
    n_chunks = N_COL_GROUPS // IN_CHUNK_GROUPS
    n_stage = D_IN_PROJ // IN_STAGE_COLS
    stage_per_chunk = cw // IN_STAGE_COLS

    def stage_copy(j):
        slot = j % STAGE_SLOTS
        return pltpu.make_async_copy(w_hbm_ref.at[:, pl.ds(j * IN_STAGE_COLS, IN_STAGE_COLS)],
                                     stage_ref.at[slot], sem_ref.at[slot])

    def normalise(x_ref, h_ref):
        for c in range(IN_TM // NORM_ROWS):
            rows = slice(c * NORM_ROWS, (c + 1) * NORM_ROWS)
            x = x_ref[rows, :]
            ms = jnp.mean(x * x, axis=-1, keepdims=True)
            h_ref[rows, :] = (x * lax.rsqrt(ms + NORM_EPS) * nw).astype(BF16)

    def store(g, a):
        a = a.astype(BF16)
        if g < COL_AQ:
            ret_ref[g] = a
        else:
            for ref, lo, hi in ((aq_ref, COL_AQ, COL_AK), (ak_ref, COL_AK, COL_AV),
                                (av_ref, COL_AV, COL_AG), (ag_ref, COL_AG, N_COL_GROUPS)):
                if lo <= g < hi:
                    ref[:, (g - lo) * HEAD_DIM:(g - lo + 1) * HEAD_DIM] = a

    def step(h_cur_ref, h_next_ref, stream_weights):
        normalise(xn_ref, h_next_ref)
        h = h_cur_ref[...]
        for c in range(n_chunks):
            if stream_weights:
                for j in range(c * stage_per_chunk, (c + 1) * stage_per_chunk):
                    stage_copy(j).wait()
                    w_ref[:, j * IN_STAGE_COLS:(j + 1) * IN_STAGE_COLS] = stage_ref[j % STAGE_SLOTS].astype(BF16)
                    if j + STAGE_SLOTS < n_stage:
                        stage_copy(j + STAGE_SLOTS).start()
            acc = jnp.dot(h, w_ref[:, c * cw:(c + 1) * cw], preferred_element_type=F32)
            for u in range(IN_CHUNK_GROUPS):
                g = c * IN_CHUNK_GROUPS + u
                a = acc[:, u * HEAD_DIM:(u + 1) * HEAD_DIM]
                kind = GROUP_KIND[g]
                if kind == RET_ROT:
                    a = a * tab_ref[TAB_RET_COS] + pltpu.roll(a, HEAD_DIM // 2, 1) * tab_ref[TAB_RET_SIN]
                elif kind == ATT_ROT_Q:
                    a = _partial_rotary(a, tab_ref[TAB_ATT_COS], tab_ref[TAB_ATT_SIN]) * ATT_Q_SCALE
                elif kind == ATT_ROT_K:
                    a = _partial_rotary(a, tab_ref[TAB_ATT_COS], tab_ref[TAB_ATT_SIN])
                elif kind == SILU:
                    a = a * jax.nn.sigmoid(a)
                store(g, a)

    @pl.when(i == 0)
    def _():
        for j in range(STAGE_SLOTS):
            stage_copy(j).start()
        normalise(x0_ref, ha_ref)
        step(ha_ref, hb_ref, True)

    @pl.when((i > 0) & (i % 2 == 0))
    def _():
        step(ha_ref, hb_ref, False)

    @pl.when(i % 2 == 1)
    def _():
        step(hb_ref, ha_ref, False)


def _in_proj(x2, norm_w, w_in, tables, seq):
    m = x2.shape[0]
    tiles_per_seq = seq // IN_TM
    assert tables.shape == (N_TABLES, seq, HEAD_DIM) and seq % IN_TM == 0 and m % seq == 0
    tok = lambda width: pl.BlockSpec((IN_TM, width), lambda i: (i, 0))
    tok_shape = lambda width: jax.ShapeDtypeStruct((m, width), BF16)
    return pl.pallas_call(
        _in_proj_kernel,
        grid=(m // IN_TM,),
        in_specs=[
            pl.BlockSpec((IN_TM, D_MODEL), lambda i: (0, 0), pipeline_mode=pl.Buffered(1)),
            pl.BlockSpec((IN_TM, D_MODEL), lambda i: (jnp.minimum(i + 1, m // IN_TM - 1), 0)),
            pl.BlockSpec((1, D_MODEL), lambda i: (0, 0)),
            pl.BlockSpec(memory_space=pl.ANY),
            pl.BlockSpec((N_TABLES, IN_TM, HEAD_DIM), lambda i: (0, i % tiles_per_seq, 0)),
        ],
        out_specs=[pl.BlockSpec((COL_AQ, IN_TM, HEAD_DIM), lambda i: (0, i, 0)),
                   tok(ATT_WIDTH), tok(ATT_KV_WIDTH), tok(ATT_KV_WIDTH), tok(ATT_WIDTH)],
        out_shape=[jax.ShapeDtypeStruct((COL_AQ, m, HEAD_DIM), BF16),
                   tok_shape(ATT_WIDTH), tok_shape(ATT_KV_WIDTH), tok_shape(ATT_KV_WIDTH), tok_shape(ATT_WIDTH)],
        scratch_shapes=[
            pltpu.VMEM((D_MODEL, D_IN_PROJ), BF16),
            pltpu.VMEM((STAGE_SLOTS, D_MODEL, IN_STAGE_COLS), F32),
            pltpu.SemaphoreType.DMA((STAGE_SLOTS,)),
            pltpu.VMEM((IN_TM, D_MODEL), BF16), pltpu.VMEM((IN_TM, D_MODEL), BF16),
        ],
        compiler_params=pltpu.CompilerParams(dimension_semantics=("arbitrary",), vmem_limit_bytes=VMEM_LIMIT),
        name="in_proj",
    )(x2, x2, norm_w.reshape(1, D_MODEL), w_in, tables)


def _log_sigmoid(z):
    return jnp.minimum(z, 0.0) - jnp.log1p(jnp.exp(-jnp.abs(z)))


def _retention_kernel(dec_ref, q_ref, k_ref, v_ref, g_ref, gnw_ref, gnb_ref, o_ref, kv_ref, st_ref, *, seq):
    C = RET_CHUNK
    nc = seq // C
    U = RET_UNROLL
    scale = HEAD_DIM ** -0.5

    lg_f = _log_sigmoid(dec_ref[0, 0:1, :])
    lg_b = _log_sigmoid(dec_ref[0, 1:2, :])
    diff = (lax.broadcasted_iota(jnp.int32, (C, C), 0) - lax.broadcasted_iota(jnp.int32, (C, C), 1)).astype(F32)
    dmat = scale * jnp.where(diff >= 0.0,
                             jnp.exp(lg_f * jnp.maximum(diff, 0.0)),
                             jnp.exp(lg_b * jnp.maximum(-diff, 0.0)))
    lg_f, lg_b = lg_f[:, :HEAD_DIM], lg_b[:, :HEAD_DIM]
    row = lax.broadcasted_iota(jnp.int32, (C, HEAD_DIM), 0).astype(F32)
    kdec_f = (scale * jnp.exp(lg_f * (C - 1.0 - row))).astype(BF16)
    kdec_b = (scale * jnp.exp(lg_b * row)).astype(BF16)
    qdec_f = jnp.exp(lg_f * (row + 1.0)).astype(BF16)
    qdec_b = jnp.exp(lg_b * (C - row)).astype(BF16)
    cdec_f = jnp.exp(lg_f * float(C))
    cdec_b = jnp.exp(lg_b * float(C))

    def chunk_rows(n):
        return pl.ds(pl.multiple_of(n * C, C), C)

    def kv_body(it, carry):
        for u in range(U):
            n = it * U + u
            rows = chunk_rows(n)
            k = k_ref[0, rows, :]
            kd = jnp.concatenate([k * kdec_f, k * kdec_b], axis=1)
            kv_ref[n] = _tn_dot(kd, v_ref[0, rows, :])
        return carry

    lax.fori_loop(0, nc // U, kv_body, 0)

    def scan_body(t, carry):
        sf, sb = carry
        m = nc - 1 - t
        st_ref[t, 0:HEAD_DIM, :] = sf.astype(BF16)
        st_ref[m, HEAD_DIM:2 * HEAD_DIM, :] = sb.astype(BF16)
        sf = sf * cdec_f + kv_ref[t, 0:HEAD_DIM, :]
        sb = sb * cdec_b + kv_ref[m, HEAD_DIM:2 * HEAD_DIM, :]
        return sf, sb

    zero = jnp.zeros((HEAD_DIM, HEAD_DIM), F32)
    lax.fori_loop(0, nc, scan_body, (zero, zero))

    gnw = gnw_ref[...]
    gnb = gnb_ref[...]

    def out_body(it, carry):
        for u in range(U):
            n = it * U + u
            rows = chunk_rows(n)
            q = q_ref[0, rows, :]
            vn = v_ref[0, rows, :]
            scores = _nt_dot(q, k_ref[0, rows, :]) * dmat
            inner = jnp.dot(scores.astype(BF16), vn, preferred_element_type=F32)
            qcat = jnp.concatenate([q * qdec_f, q * qdec_b], axis=1)
            y = inner + jnp.dot(qcat, st_ref[n], preferred_element_type=F32)
            mu = jnp.mean(y, axis=-1, keepdims=True)
            yc = y - mu
            var = jnp.mean(yc * yc, axis=-1, keepdims=True)
            yn = yc * lax.rsqrt(var + GN_EPS) * gnw + gnb
            o_ref[0, rows, :] = (yn * g_ref[0, rows, :].astype(F32)).astype(o_ref.dtype)
        return carry

    lax.fori_loop(0, nc // U, out_body, 0)


def _retention(ret, dec, gn_w, gn_b, batch, seq):
    blk = lambda off: pl.BlockSpec((1, seq, HEAD_DIM), lambda b, h: (off + h, b, 0))
    per_head = pl.BlockSpec((1, HEAD_DIM), lambda b, h: (0, h))
    return pl.pallas_call(
        functools.partial(_retention_kernel, seq=seq),
        grid=(batch, RET_HEADS),
        in_specs=[
            pl.BlockSpec((1, 8, RET_CHUNK), lambda b, h: (h, 0, 0)),
            blk(COL_RQ), blk(COL_RK), blk(COL_RV), blk(COL_RG),
            per_head, per_head,
        ],
        out_specs=pl.BlockSpec((1, seq, HEAD_DIM), lambda b, h: (h, b, 0)),
        out_shape=jax.ShapeDtypeStruct((RET_HEADS, batch * seq, HEAD_DIM), BF16),
        scratch_shapes=[
            pltpu.VMEM((seq // RET_CHUNK, 2 * HEAD_DIM, HEAD_DIM), F32),
            pltpu.VMEM((seq // RET_CHUNK, 2 * HEAD_DIM, HEAD_DIM), BF16),
        ],
        compiler_params=pltpu.CompilerParams(
            dimension_semantics=("arbitrary", "arbitrary"), vmem_limit_bytes=VMEM_LIMIT),
        name="retention",
    )(dec, ret, ret, ret, ret, gn_w.reshape(1, RET_WIDTH), gn_b.reshape(1, RET_WIDTH))


def _attention_kernel(sink_ref, q_ref, k_ref, v_ref, g0_ref, g1_ref, o_ref, vx_ref, *, seq):
    T = WINDOW
    i = pl.program_id(1)

    @pl.when(i == 0)
    def _():
        ones = jnp.ones((T, HEAD_DIM), BF16)

        def body(c, carry):
            rows = pl.ds(pl.multiple_of(c * T, T), T)
            for kh in range(ATT_KV_HEADS):
                vx_ref[rows, 2 * kh * HEAD_DIM:(2 * kh + 1) * HEAD_DIM] = v_ref[rows, kh * HEAD_DIM:(kh + 1) * HEAD_DIM]
                vx_ref[rows, (2 * kh + 1) * HEAD_DIM:(2 * kh + 2) * HEAD_DIM] = ones
            return carry

        lax.fori_loop(0, seq // T, body, 0)

    g_refs = (g0_ref, g1_ref)
    qrow = lax.broadcasted_iota(jnp.int32, (T, 3 * T), 0)
    kcol = lax.broadcasted_iota(jnp.int32, (T, 3 * T), 1)

    for t in range(ATT_TQ // T):
        rows = slice(t * T, (t + 1) * T)
        qstart = (i * (ATT_TQ // T) + t) * T
        kstart = pl.multiple_of(jnp.clip(qstart - T, 0, seq - 3 * T), T)
        bias = jnp.where(jnp.abs(qrow - kcol + (qstart - kstart)) <= WINDOW, 0.0, MASK_VALUE)
        for kh in range(ATT_KV_HEADS):
            q4 = jnp.concatenate(
                [q_ref[rows, (kh * ATT_GROUP + g) * HEAD_DIM:(kh * ATT_GROUP + g + 1) * HEAD_DIM]
                 for g in range(ATT_GROUP)], axis=0)
            kw = k_ref[pl.ds(kstart, 3 * T), kh * HEAD_DIM:(kh + 1) * HEAD_DIM]
            vw = vx_ref[pl.ds(kstart, 3 * T), 2 * kh * HEAD_DIM:(2 * kh + 2) * HEAD_DIM]
            ps, ms, sinks = [], [], []
            for g in range(ATT_GROUP):
                s = _nt_dot(q4[g * T:(g + 1) * T], kw) + bias
                sink = sink_ref[kh * ATT_GROUP + g] * LOG2E
                m = jnp.maximum(jnp.max(s, axis=-1, keepdims=True), sink)
                ps.append(jnp.exp2(s - m).astype(BF16))
                ms.append(m)
                sinks.append(sink)
            pv = jnp.dot(jnp.concatenate(ps, axis=0), vw, preferred_element_type=F32)
            for g in range(ATT_GROUP):
                blk = pv[g * T:(g + 1) * T]
                denom = blk[:, HEAD_DIM:HEAD_DIM + 1] + jnp.exp2(sinks[g] - ms[g])
                gate = g_refs[kh][rows, g * HEAD_DIM:(g + 1) * HEAD_DIM].astype(F32)
                ocols = slice((kh * ATT_GROUP + g) * HEAD_DIM, (kh * ATT_GROUP + g + 1) * HEAD_DIM)
                o_ref[rows, ocols] = (blk[:, :HEAD_DIM] / denom * gate).astype(o_ref.dtype)


def _attention(aq, ak, av, ag, sink, batch, seq):
    nq = seq // ATT_TQ
    gw = ATT_GROUP * HEAD_DIM
    return pl.pallas_call(
        functools.partial(_attention_kernel, seq=seq),
        grid=(batch, nq),
        in_specs=[
            pl.BlockSpec(memory_space=pltpu.SMEM),
            pl.BlockSpec((ATT_TQ, ATT_WIDTH), lambda b, i: (b * nq + i, 0)),
            pl.BlockSpec((seq, ATT_KV_WIDTH), lambda b, i: (b, 0)),
            pl.BlockSpec((seq, ATT_KV_WIDTH), lambda b, i: (b, 0)),
            pl.BlockSpec((ATT_TQ, gw), lambda b, i: (b * nq + i, 0)),
            pl.BlockSpec((ATT_TQ, gw), lambda b, i: (b * nq + i, 1)),
        ],
        out_specs=pl.BlockSpec((ATT_TQ, ATT_WIDTH), lambda b, i: (b * nq + i, 0)),
        out_shape=jax.ShapeDtypeStruct((batch * seq, ATT_WIDTH), BF16),
        scratch_shapes=[pltpu.VMEM((seq, 2 * ATT_KV_WIDTH), BF16)],
        compiler_params=pltpu.CompilerParams(
            dimension_semantics=("arbitrary", "arbitrary"), vmem_limit_bytes=VMEM_LIMIT),
        name="attention",
    )(sink, aq, ak, av, ag, ag)


def _out_proj_kernel(x_ref, yr_ref, ya_ref, w_hbm_ref, fw_ref, o_ref, w_ref, stage_ref, sem_ref):
    n_stage = D_MODEL // OUT_STAGE_COLS
    stage_per_chunk = OUT_CHUNK // OUT_STAGE_COLS

    def stage_copy(j):
        slot = j % STAGE_SLOTS
        return pltpu.make_async_copy(w_hbm_ref.at[:, pl.ds(j * OUT_STAGE_COLS, OUT_STAGE_COLS)],
                                     stage_ref.at[slot], sem_ref.at[slot])

    def tile(stream_weights):
        for r in range(OUT_TM // OUT_SUB):
            rows = slice(r * OUT_SUB, (r + 1) * OUT_SUB)
            y = jnp.concatenate([yr_ref[h, rows, :] for h in range(RET_HEADS)] + [ya_ref[rows, :]], axis=1)
            ssq = jnp.zeros((OUT_SUB, HEAD_DIM), F32)
            for c in range(D_MODEL // OUT_CHUNK):
                if stream_weights and r == 0:
                    for j in range(c * stage_per_chunk, (c + 1) * stage_per_chunk):
                        stage_copy(j).wait()
                        w_ref[:, j * OUT_STAGE_COLS:(j + 1) * OUT_STAGE_COLS] = stage_ref[j % STAGE_SLOTS].astype(BF16)
                        if j + STAGE_SLOTS < n_stage:
                            stage_copy(j + STAGE_SLOTS).start()
                cols = slice(c * OUT_CHUNK, (c + 1) * OUT_CHUNK)
                z = x_ref[rows, cols] + jnp.dot(y, w_ref[:, cols], preferred_element_type=F32)
                o_ref[rows, cols] = z
                for u in range(OUT_CHUNK // HEAD_DIM):
                    zu = z[:, u * HEAD_DIM:(u + 1) * HEAD_DIM]
                    ssq = ssq + zu * zu
            rs = lax.rsqrt(jnp.sum(ssq, axis=-1, keepdims=True) * (1.0 / D_MODEL) + NORM_EPS)
            for c in range(D_MODEL // OUT_CHUNK):
                cols = slice(c * OUT_CHUNK, (c + 1) * OUT_CHUNK)
                o_ref[rows, cols] = o_ref[rows, cols] * rs * fw_ref[:, cols]

    @pl.when(pl.program_id(0) == 0)
    def _():
        for j in range(STAGE_SLOTS):
            stage_copy(j).start()
        tile(True)

    @pl.when(pl.program_id(0) > 0)
    def _():
        tile(False)


def _out_proj(x2, y_ret, y_att, w_out, final_norm_w):
    m = x2.shape[0]
    return pl.pallas_call(
        _out_proj_kernel,
        grid=(m // OUT_TM,),
        in_specs=[
            pl.BlockSpec((OUT_TM, D_MODEL), lambda i: (i, 0)),
            pl.BlockSpec((RET_HEADS, OUT_TM, HEAD_DIM), lambda i: (0, i, 0)),
            pl.BlockSpec((OUT_TM, ATT_WIDTH), lambda i: (i, 0)),
            pl.BlockSpec(memory_space=pl.ANY),
            pl.BlockSpec((1, D_MODEL), lambda i: (0, 0)),
        ],
        out_specs=pl.BlockSpec((OUT_TM, D_MODEL), lambda i: (i, 0)),
        out_shape=jax.ShapeDtypeStruct((m, D_MODEL), F32),
        scratch_shapes=[
            pltpu.VMEM((RET_WIDTH + ATT_WIDTH, D_MODEL), BF16),
            pltpu.VMEM((STAGE_SLOTS, RET_WIDTH + ATT_WIDTH, OUT_STAGE_COLS), F32),
            pltpu.SemaphoreType.DMA((STAGE_SLOTS,)),
        ],
        compiler_params=pltpu.CompilerParams(
            dimension_semantics=("arbitrary",), vmem_limit_bytes=VMEM_LIMIT),
        name="out_proj",
    )(x2, y_ret, y_att, w_out, final_norm_w.reshape(1, D_MODEL))


def _rotary_tables(seq):
    def cos_sin(theta, half):
        inv_freq = (theta ** (-jnp.arange(half, dtype=F32) / half))[None, :]
        hi = jnp.arange(seq // TRIG_SPLIT, dtype=F32)[:, None] * TRIG_SPLIT * inv_freq
        lo = jnp.arange(TRIG_SPLIT, dtype=F32)[:, None] * inv_freq
        ch, sh, cl, sl = jnp.cos(hi)[:, None], jnp.sin(hi)[:, None], jnp.cos(lo)[None], jnp.sin(lo)[None]
        return (ch * cl - sh * sl).reshape(seq, half), (sh * cl + ch * sl).reshape(seq, half)

    cos_r, sin_r = cos_sin(RET_THETA, HEAD_DIM // 2)
    cos_a, sin_a = cos_sin(ROPE_THETA, ROPE_DIMS // 2)
    rest = jnp.zeros((seq, HEAD_DIM - ROPE_DIMS), F32)
    tabs = [None] * N_TABLES
    tabs[TAB_RET_COS] = jnp.concatenate([cos_r, cos_r], axis=1)
    tabs[TAB_RET_SIN] = jnp.concatenate([-sin_r, sin_r], axis=1)
    tabs[TAB_ATT_COS] = jnp.concatenate([cos_a, cos_a, rest + 1.0], axis=1)
    tabs[TAB_ATT_SIN] = jnp.concatenate([-sin_a, sin_a, rest], axis=1)
    return jnp.stack(tabs)


def _layer(x2, w_in, w_out, norm_w, decay_fwd, decay_bwd, gn_w, gn_b, sink, post_norm_w, batch, seq):
    ret, aq, ak, av, ag = _in_proj(x2, norm_w, w_in.astype(F32), _rotary_tables(seq), seq)
    dec = jnp.zeros((RET_HEADS, 8, RET_CHUNK), F32)
    dec = dec.at[:, 0, :].set(decay_fwd.astype(F32)[:, None])
    dec = dec.at[:, 1, :].set(decay_bwd.astype(F32)[:, None])
    y_ret = _retention(ret, dec, gn_w.astype(F32), gn_b.astype(F32), batch, seq)
    y_att = _attention(aq, ak, av, ag, sink.astype(F32), batch, seq)
    return _out_proj(x2, y_ret, y_att, w_out.astype(F32), post_norm_w)


def kernel(x, w_in, w_out, norm_w, decay_fwd, decay_bwd, gn_w, gn_b, sink, final_norm_w):
    batch, seq, _ = x.shape
    depth = w_in.shape[0]
    assert depth == 1, "the final RMSNorm is fused into the single layer's output projection"
    assert seq % (RET_UNROLL * RET_CHUNK) == 0 and seq % ATT_TQ == 0 and seq >= 3 * WINDOW
    assert (batch * seq) % OUT_TM == 0 and N_COL_GROUPS % IN_CHUNK_GROUPS == 0 and seq % TRIG_SPLIT == 0
    x2 = x.reshape(batch * seq, D_MODEL)
    out = _layer(x2, w_in[0], w_out[0], norm_w[0], decay_fwd[0], decay_bwd[0], gn_w[0], gn_b[0], sink[0],
                 final_norm_w, batch, seq)
    return out.reshape(batch, seq, D_MODEL)
```

```python
import functools

import jax
import jax.numpy as jnp
from jax import lax
from jax.experimental import pallas as pl
from jax.experimental.pallas import tpu as pltpu

F32 = jnp.float32
BF16 = jnp.bfloat16

D_MODEL = 2048
HEAD_DIM = 128
RET_HEADS = 8
RET_WIDTH = RET_HEADS * HEAD_DIM
RET_CHUNK = 256
RET_THETA = 10000.0
GN_EPS = 1e-5
ATT_HEADS = 8
ATT_KV_HEADS = 2
ATT_GROUP = ATT_HEADS // ATT_KV_HEADS
ATT_WIDTH = ATT_HEADS * HEAD_DIM
ATT_KV_WIDTH = ATT_KV_HEADS * HEAD_DIM
WINDOW = 128
ROPE_THETA = 500000.0
ROPE_DIMS = HEAD_DIM // 4
NORM_EPS = 1e-6
MASK_VALUE = -1e30
LOG2E = 1.4426950408889634
D_IN_PROJ = 4 * RET_WIDTH + 2 * ATT_WIDTH + 2 * ATT_KV_WIDTH

COL_RQ = 0
COL_RK = COL_RQ + RET_HEADS
COL_RV = COL_RK + RET_HEADS
COL_RG = COL_RV + RET_HEADS
COL_AQ = COL_RG + RET_HEADS
COL_AK = COL_AQ + ATT_HEADS
COL_AV = COL_AK + ATT_KV_HEADS
COL_AG = COL_AV + ATT_KV_HEADS
N_COL_GROUPS = COL_AG + ATT_HEADS

PLAIN, RET_ROT, ATT_ROT_Q, ATT_ROT_K, SILU = range(5)
GROUP_KIND = ([RET_ROT] * (2 * RET_HEADS) + [PLAIN] * RET_HEADS + [SILU] * RET_HEADS
              + [ATT_ROT_Q] * ATT_HEADS + [ATT_ROT_K] * ATT_KV_HEADS + [PLAIN] * ATT_KV_HEADS + [SILU] * ATT_HEADS)
assert len(GROUP_KIND) == N_COL_GROUPS
TAB_RET_COS, TAB_RET_SIN, TAB_ATT_COS, TAB_ATT_SIN, N_TABLES = 0, 1, 2, 3, 4
ATT_Q_SCALE = (HEAD_DIM ** -0.5) * LOG2E
TRIG_SPLIT = 64

IN_TM = 256
IN_CHUNK_GROUPS = 4
NORM_ROWS = 128
IN_STAGE_COLS = 512
IN_STAGE_SLOTS = 2
RET_UNROLL = 16
ATT_TQ = 1024
OUT_TM = 512
OUT_STAGE_COLS = 256
OUT_STAGE_SLOTS = 4
OUT_SUB = 256
OUT_CHUNK = 512
VMEM_LIMIT = 56 * 1024 * 1024


def _nt_dot(a, b):
    return lax.dot_general(a, b, (((1,), (1,)), ((), ())), preferred_element_type=F32)


def _tn_dot(a, b):
    return lax.dot_general(a, b, (((0,), (0,)), ((), ())), preferred_element_type=F32)


def _partial_rotary(x, cos, sin_signed):
    half = ROPE_DIMS // 2
    lane = lax.broadcasted_iota(jnp.int32, x.shape, 1)
    partner = jnp.where(lane < half,
                        pltpu.roll(x, HEAD_DIM - half, 1),
                        pltpu.roll(x, half, 1))
    return x * cos + partner * sin_signed


def _in_proj_kernel(x0_ref, xn_ref, nw_ref, w_hbm_ref, tab_ref, ret_ref, aq_ref, ak_ref, av_ref, ag_ref,
                    w_ref, stage_ref, sem_ref, ha_ref, hb_ref):
    i = pl.program_id(0)
    nw = nw_ref[...]
    cw = IN_CHUNK_GROUPS * HEAD_DIM
    n_chunks = N_COL_GROUPS // IN_CHUNK_GROUPS
    n_stage = D_IN_PROJ // IN_STAGE_COLS
    stage_per_chunk = cw // IN_STAGE_COLS

    def stage_copy(j):
        slot = j % IN_STAGE_SLOTS
        return pltpu.make_async_copy(w_hbm_ref.at[:, pl.ds(j * IN_STAGE_COLS, IN_STAGE_COLS)],
                                     stage_ref.at[slot], sem_ref.at[slot])

    def normalise(x_ref, h_ref):
        for c in range(IN_TM // NORM_ROWS):
            rows = slice(c * NORM_ROWS, (c + 1) * NORM_ROWS)
            x = x_ref[rows, :]
            ms = jnp.mean(x * x, axis=-1, keepdims=True)
            h_ref[rows, :] = (x * lax.rsqrt(ms + NORM_EPS) * nw).astype(BF16)

    def store(g, a):
        a = a.astype(BF16)
        if g < COL_AQ:
            ret_ref[g] = a
        else:
            for ref, lo, hi in ((aq_ref, COL_AQ, COL_AK), (ak_ref, COL_AK, COL_AV),
                                (av_ref, COL_AV, COL_AG), (ag_ref, COL_AG, N_COL_GROUPS)):
                if lo <= g < hi:
                    ref[:, (g - lo) * HEAD_DIM:(g - lo + 1) * HEAD_DIM] = a

    def step(h_cur_ref, h_next_ref, stream_weights):
        normalise(xn_ref, h_next_ref)
        h = h_cur_ref[...]
        for c in range(n_chunks):
            if stream_weights:
                for j in range(c * stage_per_chunk, (c + 1) * stage_per_chunk):
                    stage_copy(j).wait()
                    w_ref[:, j * IN_STAGE_COLS:(j + 1) * IN_STAGE_COLS] = stage_ref[j % IN_STAGE_SLOTS].astype(BF16)
                    if j + IN_STAGE_SLOTS < n_stage:
                        stage_copy(j + IN_STAGE_SLOTS).start()
            acc = jnp.dot(h, w_ref[:, c * cw:(c + 1) * cw], preferred_element_type=F32)
            for u in range(IN_CHUNK_GROUPS):
                g = c * IN_CHUNK_GROUPS + u
                a = acc[:, u * HEAD_DIM:(u + 1) * HEAD_DIM]
                kind = GROUP_KIND[g]
                if kind == RET_ROT:
                    a = a * tab_ref[TAB_RET_COS] + pltpu.roll(a, HEAD_DIM // 2, 1) * tab_ref[TAB_RET_SIN]
                elif kind == ATT_ROT_Q:
                    a = _partial_rotary(a, tab_ref[TAB_ATT_COS], tab_ref[TAB_ATT_SIN]) * ATT_Q_SCALE
                elif kind == ATT_ROT_K:
                    a = _partial_rotary(a, tab_ref[TAB_ATT_COS], tab_ref[TAB_ATT_SIN])
                elif kind == SILU:
                    a = a * jax.nn.sigmoid(a)
                store(g, a)

    @pl.when(i == 0)
    def _():
        for j in range(IN_STAGE_SLOTS):
            stage_copy(j).start()
        normalise(x0_ref, ha_ref)
        step(ha_ref, hb_ref, True)

    @pl.when((i > 0) & (i % 2 == 0))
    def _():
        step(ha_ref, hb_ref, False)

    @pl.when(i % 2 == 1)
    def _():
        step(hb_ref, ha_ref, False)


def _in_proj(x2, norm_w, w_in, tables, seq):
    m = x2.shape[0]
    tiles_per_seq = seq // IN_TM
    assert tables.shape == (N_TABLES, seq, HEAD_DIM) and seq % IN_TM == 0 and m % seq == 0
    tok = lambda width: pl.BlockSpec((IN_TM, width), lambda i: (i, 0))
    tok_shape = lambda width: jax.ShapeDtypeStruct((m, width), BF16)
    return pl.pallas_call(
        _in_proj_kernel,
        grid=(m // IN_TM,),
        in_specs=[
            pl.BlockSpec((IN_TM, D_MODEL), lambda i: (0, 0), pipeline_mode=pl.Buffered(1)),
            pl.BlockSpec((IN_TM, D_MODEL), lambda i: (jnp.minimum(i + 1, m // IN_TM - 1), 0)),
            pl.BlockSpec((1, D_MODEL), lambda i: (0, 0)),
            pl.BlockSpec(memory_space=pl.ANY),
            pl.BlockSpec((N_TABLES, IN_TM, HEAD_DIM), lambda i: (0, i % tiles_per_seq, 0)),
        ],
        out_specs=[pl.BlockSpec((COL_AQ, IN_TM, HEAD_DIM), lambda i: (0, i, 0)),
                   tok(ATT_WIDTH), tok(ATT_KV_WIDTH), tok(ATT_KV_WIDTH), tok(ATT_WIDTH)],
        out_shape=[jax.ShapeDtypeStruct((COL_AQ, m, HEAD_DIM), BF16),
                   tok_shape(ATT_WIDTH), tok_shape(ATT_KV_WIDTH), tok_shape(ATT_KV_WIDTH), tok_shape(ATT_WIDTH)],
        scratch_shapes=[
            pltpu.VMEM((D_MODEL, D_IN_PROJ), BF16),
            pltpu.VMEM((IN_STAGE_SLOTS, D_MODEL, IN_STAGE_COLS), F32),
            pltpu.SemaphoreType.DMA((IN_STAGE_SLOTS,)),
            pltpu.VMEM((IN_TM, D_MODEL), BF16), pltpu.VMEM((IN_TM, D_MODEL), BF16),
        ],
        compiler_params=pltpu.CompilerParams(dimension_semantics=("arbitrary",), vmem_limit_bytes=VMEM_LIMIT),
        name="in_proj",
    )(x2, x2, norm_w.reshape(1, D_MODEL), w_in, tables)


def _log_sigmoid(z):
    return jnp.minimum(z, 0.0) - jnp.log1p(jnp.exp(-jnp.abs(z)))


def _retention_kernel(dec_ref, q_ref, k_ref, v_ref, g_ref, gnw_ref, gnb_ref, o_ref, kv_ref, st_ref, *, seq):
    C = RET_CHUNK
    nc = seq // C
    U = RET_UNROLL
    scale = HEAD_DIM ** -0.5

    lg_f = _log_sigmoid(dec_ref[0, 0:1, :])
    lg_b = _log_sigmoid(dec_ref[0, 1:2, :])
    diff = (lax.broadcasted_iota(jnp.int32, (C, C), 0) - lax.broadcasted_iota(jnp.int32, (C, C), 1)).astype(F32)
    dmat = scale * jnp.where(diff >= 0.0,
                             jnp.exp(lg_f * jnp.maximum(diff, 0.0)),
                             jnp.exp(lg_b * jnp.maximum(-diff, 0.0)))
    lg_f, lg_b = lg_f[:, :HEAD_DIM], lg_b[:, :HEAD_DIM]
    row = lax.broadcasted_iota(jnp.int32, (C, HEAD_DIM), 0).astype(F32)
    kdec_f = (scale * jnp.exp(lg_f * (C - 1.0 - row))).astype(BF16)
    kdec_b = (scale * jnp.exp(lg_b * row)).astype(BF16)
    qdec_f = jnp.exp(lg_f * (row + 1.0)).astype(BF16)
    qdec_b = jnp.exp(lg_b * (C - row)).astype(BF16)
    cdec_f = jnp.exp(lg_f * float(C))
    cdec_b = jnp.exp(lg_b * float(C))

    def chunk_rows(n):
        return pl.ds(pl.multiple_of(n * C, C), C)

    def kv_body(it, carry):
        for u in range(U):
            n = it * U + u
            rows = chunk_rows(n)
            k = k_ref[0, rows, :]
            kd = jnp.concatenate([k * kdec_f, k * kdec_b], axis=1)
            kv_ref[n] = _tn_dot(kd, v_ref[0, rows, :])
        return carry

    lax.fori_loop(0, nc // U, kv_body, 0)

    def scan_body(t, carry):
        sf, sb = carry
        m = nc - 1 - t
        st_ref[t, 0:HEAD_DIM, :] = sf.astype(BF16)
        st_ref[m, HEAD_DIM:2 * HEAD_DIM, :] = sb.astype(BF16)
        sf = sf * cdec_f + kv_ref[t, 0:HEAD_DIM, :]
        sb = sb * cdec_b + kv_ref[m, HEAD_DIM:2 * HEAD_DIM, :]
        return sf, sb

    zero = jnp.zeros((HEAD_DIM, HEAD_DIM), F32)
    lax.fori_loop(0, nc, scan_body, (zero, zero))

    gnw = gnw_ref[...]
    gnb = gnb_ref[...]

    def out_body(it, carry):
        for u in range(U):
            n = it * U + u
            rows = chunk_rows(n)
            q = q_ref[0, rows, :]
            vn = v_ref[0, rows, :]
            scores = _nt_dot(q, k_ref[0, rows, :]) * dmat
            inner = jnp.dot(scores.astype(BF16), vn, preferred_element_type=F32)
            qcat = jnp.concatenate([q * qdec_f, q * qdec_b], axis=1)
            y = inner + jnp.dot(qcat, st_ref[n], preferred_element_type=F32)
            mu = jnp.mean(y, axis=-1, keepdims=True)
            yc = y - mu
            var = jnp.mean(yc * yc, axis=-1, keepdims=True)
            yn = yc * lax.rsqrt(var + GN_EPS) * gnw + gnb
            o_ref[0, rows, :] = (yn * g_ref[0, rows, :].astype(F32)).astype(o_ref.dtype)
        return carry

    lax.fori_loop(0, nc // U, out_body, 0)


def _retention(ret, dec, gn_w, gn_b, batch, seq):
    blk = lambda off: pl.BlockSpec((1, seq, HEAD_DIM), lambda b, h: (off + h, b, 0))
    per_head = pl.BlockSpec((1, HEAD_DIM), lambda b, h: (0, h))
    return pl.pallas_call(
        functools.partial(_retention_kernel, seq=seq),
        grid=(batch, RET_HEADS),
        in_specs=[
            pl.BlockSpec((1, 8, RET_CHUNK), lambda b, h: (h, 0, 0)),
            blk(COL_RQ), blk(COL_RK), blk(COL_RV), blk(COL_RG),
            per_head, per_head,
        ],
        out_specs=pl.BlockSpec((1, seq, HEAD_DIM), lambda b, h: (h, b, 0)),
        out_shape=jax.ShapeDtypeStruct((RET_HEADS, batch * seq, HEAD_DIM), BF16),
        scratch_shapes=[
            pltpu.VMEM((seq // RET_CHUNK, 2 * HEAD_DIM, HEAD_DIM), F32),
            pltpu.VMEM((seq // RET_CHUNK, 2 * HEAD_DIM, HEAD_DIM), BF16),
        ],
        compiler_params=pltpu.CompilerParams(
            dimension_semantics=("arbitrary", "arbitrary"), vmem_limit_bytes=VMEM_LIMIT),
        name="retention",
    )(dec, ret, ret, ret, ret, gn_w.reshape(1, RET_WIDTH), gn_b.reshape(1, RET_WIDTH))


def _attention_kernel(sink_ref, q_ref, k_ref, v_ref, g0_ref, g1_ref, o_ref, vx_ref, *, seq):
    T = WINDOW
    i = pl.program_id(1)

    @pl.when(i == 0)
    def _():
        ones = jnp.ones((T, HEAD_DIM), BF16)

        def body(c, carry):
            rows = pl.ds(pl.multiple_of(c * T, T), T)
            for kh in range(ATT_KV_HEADS):
                vx_ref[rows, 2 * kh * HEAD_DIM:(2 * kh + 1) * HEAD_DIM] = v_ref[rows, kh * HEAD_DIM:(kh + 1) * HEAD_DIM]
                vx_ref[rows, (2 * kh + 1) * HEAD_DIM:(2 * kh + 2) * HEAD_DIM] = ones
            return carry

        lax.fori_loop(0, seq // T, body, 0)

    g_refs = (g0_ref, g1_ref)
    qrow = lax.broadcasted_iota(jnp.int32, (T, 3 * T), 0)
    kcol = lax.broadcasted_iota(jnp.int32, (T, 3 * T), 1)

    for t in range(ATT_TQ // T):
        rows = slice(t * T, (t + 1) * T)
        qstart = (i * (ATT_TQ // T) + t) * T
        kstart = pl.multiple_of(jnp.clip(qstart - T, 0, seq - 3 * T), T)
        bias = jnp.where(jnp.abs(qrow - kcol + (qstart - kstart)) <= WINDOW, 0.0, MASK_VALUE)
        for kh in range(ATT_KV_HEADS):
            q4 = jnp.concatenate(
                [q_ref[rows, (kh * ATT_GROUP + g) * HEAD_DIM:(kh * ATT_GROUP + g + 1) * HEAD_DIM]
                 for g in range(ATT_GROUP)], axis=0)
            kw = k_ref[pl.ds(kstart, 3 * T), kh * HEAD_DIM:(kh + 1) * HEAD_DIM]
            vw = vx_ref[pl.ds(kstart, 3 * T), 2 * kh * HEAD_DIM:(2 * kh + 2) * HEAD_DIM]
            ps, ms, sinks = [], [], []
            for g in range(ATT_GROUP):
                s = _nt_dot(q4[g * T:(g + 1) * T], kw) + bias
                sink = sink_ref[kh * ATT_GROUP + g] * LOG2E
                m = jnp.maximum(jnp.max(s, axis=-1, keepdims=True), sink)
                ps.append(jnp.exp2(s - m).astype(BF16))
                ms.append(m)
                sinks.append(sink)
            pv = jnp.dot(jnp.concatenate(ps, axis=0), vw, preferred_element_type=F32)
            for g in range(ATT_GROUP):
                blk = pv[g * T:(g + 1) * T]
                denom = blk[:, HEAD_DIM:HEAD_DIM + 1] + jnp.exp2(sinks[g] - ms[g])
                gate = g_refs[kh][rows, g * HEAD_DIM:(g + 1) * HEAD_DIM].astype(F32)
                ocols = slice((kh * ATT_GROUP + g) * HEAD_DIM, (kh * ATT_GROUP + g + 1) * HEAD_DIM)
                o_ref[rows, ocols] = (blk[:, :HEAD_DIM] / denom * gate).astype(o_ref.dtype)


def _attention(aq, ak, av, ag, sink, batch, seq):
    nq = seq // ATT_TQ
    gw = ATT_GROUP * HEAD_DIM
    return pl.pallas_call(
        functools.partial(_attention_kernel, seq=seq),
        grid=(batch, nq),
        in_specs=[
            pl.BlockSpec(memory_space=pltpu.SMEM),
            pl.BlockSpec((ATT_TQ, ATT_WIDTH), lambda b, i: (b * nq + i, 0)),
            pl.BlockSpec((seq, ATT_KV_WIDTH), lambda b, i: (b, 0)),
            pl.BlockSpec((seq, ATT_KV_WIDTH), lambda b, i: (b, 0)),
            pl.BlockSpec((ATT_TQ, gw), lambda b, i: (b * nq + i, 0)),
            pl.BlockSpec((ATT_TQ, gw), lambda b, i: (b * nq + i, 1)),
        ],
        out_specs=pl.BlockSpec((ATT_TQ, ATT_WIDTH), lambda b, i: (b * nq + i, 0)),
        out_shape=jax.ShapeDtypeStruct((batch * seq, ATT_WIDTH), BF16),
        scratch_shapes=[pltpu.VMEM((seq, 2 * ATT_KV_WIDTH), BF16)],
        compiler_params=pltpu.CompilerParams(
            dimension_semantics=("arbitrary", "arbitrary"), vmem_limit_bytes=VMEM_LIMIT),
        name="attention",
    )(sink, aq, ak, av, ag, ag)


def _out_proj_kernel(x_ref, yr_ref, ya_ref, w_hbm_ref, fw_ref, o_ref, w_ref, stage_ref, sem_ref):
    n_stage = D_MODEL // OUT_STAGE_COLS
    stage_per_chunk = OUT_CHUNK // OUT_STAGE_COLS

    def stage_copy(j):
        slot = j % OUT_STAGE_SLOTS
        return pltpu.make_async_copy(w_hbm_ref.at[:, pl.ds(j * OUT_STAGE_COLS, OUT_STAGE_COLS)],
                                     stage_ref.at[slot], sem_ref.at[slot])

    def tile(stream_weights):
        for r in range(OUT_TM // OUT_SUB):
            rows = slice(r * OUT_SUB, (r + 1) * OUT_SUB)
            y = jnp.concatenate([yr_ref[h, rows, :] for h in range(RET_HEADS)] + [ya_ref[rows, :]], axis=1)
            ssq = jnp.zeros((OUT_SUB, HEAD_DIM), F32)
            for c in range(D_MODEL // OUT_CHUNK):
                if stream_weights and r == 0:
                    for j in range(c * stage_per_chunk, (c + 1) * stage_per_chunk):
                        stage_copy(j).wait()
                        w_ref[:, j * OUT_STAGE_COLS:(j + 1) * OUT_STAGE_COLS] = stage_ref[j % OUT_STAGE_SLOTS].astype(BF16)
                        if j + OUT_STAGE_SLOTS < n_stage:
                            stage_copy(j + OUT_STAGE_SLOTS).start()
                cols = slice(c * OUT_CHUNK, (c + 1) * OUT_CHUNK)
                z = x_ref[rows, cols] + jnp.dot(y, w_ref[:, cols], preferred_element_type=F32)
                o_ref[rows, cols] = z
                for u in range(OUT_CHUNK // HEAD_DIM):
                    zu = z[:, u * HEAD_DIM:(u + 1) * HEAD_DIM]
                    ssq = ssq + zu * zu
            rs = lax.rsqrt(jnp.sum(ssq, axis=-1, keepdims=True) * (1.0 / D_MODEL) + NORM_EPS)
            for c in range(D_MODEL // OUT_CHUNK):
                cols = slice(c * OUT_CHUNK, (c + 1) * OUT_CHUNK)
                o_ref[rows, cols] = o_ref[rows, cols] * rs * fw_ref[:, cols]

    @pl.when(pl.program_id(0) == 0)
    def _():
        for j in range(OUT_STAGE_SLOTS):
            stage_copy(j).start()
        tile(True)

    @pl.when(pl.program_id(0) > 0)
    def _():
        tile(False)


def _out_proj(x2, y_ret, y_att, w_out, final_norm_w):
    m = x2.shape[0]
    return pl.pallas_call(
        _out_proj_kernel,
        grid=(m // OUT_TM,),
        in_specs=[
            pl.BlockSpec((OUT_TM, D_MODEL), lambda i: (i, 0)),
            pl.BlockSpec((RET_HEADS, OUT_TM, HEAD_DIM), lambda i: (0, i, 0)),
            pl.BlockSpec((OUT_TM, ATT_WIDTH), lambda i: (i, 0)),
            pl.BlockSpec(memory_space=pl.ANY),
            pl.BlockSpec((1, D_MODEL), lambda i: (0, 0)),
        ],
        out_specs=pl.BlockSpec((OUT_TM, D_MODEL), lambda i: (i, 0)),
        out_shape=jax.ShapeDtypeStruct((m, D_MODEL), F32),
        scratch_shapes=[
            pltpu.VMEM((RET_WIDTH + ATT_WIDTH, D_MODEL), BF16),
            pltpu.VMEM((OUT_STAGE_SLOTS, RET_WIDTH + ATT_WIDTH, OUT_STAGE_COLS), F32),
            pltpu.SemaphoreType.DMA((OUT_STAGE_SLOTS,)),
        ],
        compiler_params=pltpu.CompilerParams(
            dimension_semantics=("arbitrary",), vmem_limit_bytes=VMEM_LIMIT),
        name="out_proj",
    )(x2, y_ret, y_att, w_out, final_norm_w.reshape(1, D_MODEL))


def _rotary_tables(seq):
    def cos_sin(theta, half):
        inv_freq = (theta ** (-jnp.arange(half, dtype=F32) / half))[None, :]
        hi = jnp.arange(seq // TRIG_SPLIT, dtype=F32)[:, None] * TRIG_SPLIT * inv_freq
        lo = jnp.arange(TRIG_SPLIT, dtype=F32)[:, None] * inv_freq
        ch, sh, cl, sl = jnp.cos(hi)[:, None], jnp.sin(hi)[:, None], jnp.cos(lo)[None], jnp.sin(lo)[None]
        return (ch * cl - sh * sl).reshape(seq, half), (sh * cl + ch * sl).reshape(seq, half)

    cos_r, sin_r = cos_sin(RET_THETA, HEAD_DIM // 2)
    cos_a, sin_a = cos_sin(ROPE_THETA, ROPE_DIMS // 2)
    rest = jnp.zeros((seq, HEAD_DIM - ROPE_DIMS), F32)
    tabs = [None] * N_TABLES
    tabs[TAB_RET_COS] = jnp.concatenate([cos_r, cos_r], axis=1)
    tabs[TAB_RET_SIN] = jnp.concatenate([-sin_r, sin_r], axis=1)
    tabs[TAB_ATT_COS] = jnp.concatenate([cos_a, cos_a, rest + 1.0], axis=1)
    tabs[TAB_ATT_SIN] = jnp.concatenate([-sin_a, sin_a, rest], axis=1)
    return jnp.stack(tabs)


def _layer(x2, w_in, w_out, norm_w, decay_fwd, decay_bwd, gn_w, gn_b, sink, post_norm_w, batch, seq):
    ret, aq, ak, av, ag = _in_proj(x2, norm_w, w_in.astype(F32), _rotary_tables(seq), seq)
    dec = jnp.zeros((RET_HEADS, 8, RET_CHUNK), F32)
    dec = dec.at[:, 0, :].set(decay_fwd.astype(F32)[:, None])
    dec = dec.at[:, 1, :].set(decay_bwd.astype(F32)[:, None])
    y_ret = _retention(ret, dec, gn_w.astype(F32), gn_b.astype(F32), batch, seq)
    y_att = _attention(aq, ak, av, ag, sink.astype(F32), batch, seq)
    return _out_proj(x2, y_ret, y_att, w_out.astype(F32), post_norm_w)


def kernel(x, w_in, w_out, norm_w, decay_fwd, decay_bwd, gn_w, gn_b, sink, final_norm_w):
    batch, seq, _ = x.shape
    depth = w_in.shape[0]
    assert depth == 1, "the final RMSNorm is fused into the single layer's output projection"
    assert seq % (RET_UNROLL * RET_CHUNK) == 0 and seq % ATT_TQ == 0 and seq >= 3 * WINDOW
    assert (batch * seq) % OUT_TM == 0 and N_COL_GROUPS % IN_CHUNK_GROUPS == 0 and seq % TRIG_SPLIT == 0
    x2 = x.reshape(batch * seq, D_MODEL)
    out = _layer(x2, w_in[0], w_out[0], norm_w[0], decay_fwd[0], decay_bwd[0], gn_w[0], gn_b[0], sink[0],
                 final_norm_w, batch, seq)
    return out.reshape(batch, seq, D_MODEL)
```

```python
import functools

import jax
import jax.numpy as jnp
from jax import lax
from jax.experimental import pallas as pl
from jax.experimental.pallas import tpu as pltpu

F32 = jnp.float32
BF16 = jnp.bfloat16

D_MODEL = 2048
HEAD_DIM = 128
RET_HEADS = 8
RET_WIDTH = RET_HEADS * HEAD_DIM
RET_CHUNK = 256
RET_THETA = 10000.0
GN_EPS = 1e-5
ATT_HEADS = 8
ATT_KV_HEADS = 2
ATT_GROUP = ATT_HEADS // ATT_KV_HEADS
ATT_WIDTH = ATT_HEADS * HEAD_DIM
ATT_KV_WIDTH = ATT_KV_HEADS * HEAD_DIM
WINDOW = 128
ROPE_THETA = 500000.0
ROPE_DIMS = HEAD_DIM // 4
NORM_EPS = 1e-6
MASK_VALUE = -1e30
LOG2E = 1.4426950408889634
D_IN_PROJ = 4 * RET_WIDTH + 2 * ATT_WIDTH + 2 * ATT_KV_WIDTH

COL_RQ = 0
COL_RK = COL_RQ + RET_HEADS
COL_RV = COL_RK + RET_HEADS
COL_RG = COL_RV + RET_HEADS
COL_AQ = COL_RG + RET_HEADS
COL_AK = COL_AQ + ATT_HEADS
COL_AV = COL_AK + ATT_KV_HEADS
COL_AG = COL_AV + ATT_KV_HEADS
N_COL_GROUPS = COL_AG + ATT_HEADS

PLAIN, RET_ROT, ATT_ROT_Q, ATT_ROT_K, SILU = range(5)
GROUP_KIND = ([RET_ROT] * (2 * RET_HEADS) + [PLAIN] * RET_HEADS + [SILU] * RET_HEADS
              + [ATT_ROT_Q] * ATT_HEADS + [ATT_ROT_K] * ATT_KV_HEADS + [PLAIN] * ATT_KV_HEADS + [SILU] * ATT_HEADS)
assert len(GROUP_KIND) == N_COL_GROUPS
TAB_RET_COS, TAB_RET_SIN, TAB_ATT_COS, TAB_ATT_SIN, N_TABLES = 0, 1, 2, 3, 4
ATT_Q_SCALE = (HEAD_DIM ** -0.5) * LOG2E
TRIG_SPLIT = 64

IN_TM = 256
IN_CHUNK_GROUPS = 4
NORM_ROWS = 128
IN_STAGE_COLS = 512
IN_STAGE_SLOTS = 2
RET_UNROLL = 16
ATT_TQ = 2048
OUT_TM = 512
OUT_STAGE_COLS = 256
OUT_STAGE_SLOTS = 4
OUT_SUB = 256
OUT_CHUNK = 512
VMEM_LIMIT = 56 * 1024 * 1024


def _nt_dot(a, b):
    return lax.dot_general(a, b, (((1,), (1,)), ((), ())), preferred_element_type=F32)


def _tn_dot(a, b):
    return lax.dot_general(a, b, (((0,), (0,)), ((), ())), preferred_element_type=F32)


def _partial_rotary(x, cos, sin_signed):
    half = ROPE_DIMS // 2
    lane = lax.broadcasted_iota(jnp.int32, x.shape, 1)
    partner = jnp.where(lane < half,
                        pltpu.roll(x, HEAD_DIM - half, 1),
                        pltpu.roll(x, half, 1))
    return x * cos + partner * sin_signed


def _in_proj_kernel(x0_ref, xn_ref, nw_ref, w_hbm_ref, tab_ref, ret_ref, aq_ref, ak_ref, av_ref, ag_ref,
                    w_ref, stage_ref, sem_ref, ha_ref, hb_ref):
    i = pl.program_id(0)
    nw = nw_ref[...]
    cw = IN_CHUNK_GROUPS * HEAD_DIM
    n_chunks = N_COL_GROUPS // IN_CHUNK_GROUPS
    n_stage = D_IN_PROJ // IN_STAGE_COLS
    stage_per_chunk = cw // IN_STAGE_COLS

    def stage_copy(j):
        slot = j % IN_STAGE_SLOTS
        return pltpu.make_async_copy(w_hbm_ref.at[:, pl.ds(j * IN_STAGE_COLS, IN_STAGE_COLS)],
                                     stage_ref.at[slot], sem_ref.at[slot])

    def normalise(x_ref, h_ref):
        for c in range(IN_TM // NORM_ROWS):
            rows = slice(c * NORM_ROWS, (c + 1) * NORM_ROWS)
            x = x_ref[rows, :]
            ms = jnp.mean(x * x, axis=-1, keepdims=True)
            h_ref[rows, :] = (x * lax.rsqrt(ms + NORM_EPS) * nw).astype(BF16)

    def store(g, a):
        a = a.astype(BF16)
        if g < COL_AQ:
            ret_ref[g] = a
        else:
            for ref, lo, hi in ((aq_ref, COL_AQ, COL_AK), (ak_ref, COL_AK, COL_AV),
                                (av_ref, COL_AV, COL_AG), (ag_ref, COL_AG, N_COL_GROUPS)):
                if lo <= g < hi:
                    ref[:, (g - lo) * HEAD_DIM:(g - lo + 1) * HEAD_DIM] = a

    def step(h_cur_ref, h_next_ref, stream_weights):
        normalise(xn_ref, h_next_ref)
        h = h_cur_ref[...]
        for c in range(n_chunks):
            if stream_weights:
                for j in range(c * stage_per_chunk, (c + 1) * stage_per_chunk):
                    stage_copy(j).wait()
                    w_ref[:, j * IN_STAGE_COLS:(j + 1) * IN_STAGE_COLS] = stage_ref[j % IN_STAGE_SLOTS].astype(BF16)
                    if j + IN_STAGE_SLOTS < n_stage:
                        stage_copy(j + IN_STAGE_SLOTS).start()
            acc = jnp.dot(h, w_ref[:, c * cw:(c + 1) * cw], preferred_element_type=F32)
            for u in range(IN_CHUNK_GROUPS):
                g = c * IN_CHUNK_GROUPS + u
                a = acc[:, u * HEAD_DIM:(u + 1) * HEAD_DIM]
                kind = GROUP_KIND[g]
                if kind == RET_ROT:
                    a = a * tab_ref[TAB_RET_COS] + pltpu.roll(a, HEAD_DIM // 2, 1) * tab_ref[TAB_RET_SIN]
                elif kind == ATT_ROT_Q:
                    a = _partial_rotary(a, tab_ref[TAB_ATT_COS], tab_ref[TAB_ATT_SIN]) * ATT_Q_SCALE
                elif kind == ATT_ROT_K:
                    a = _partial_rotary(a, tab_ref[TAB_ATT_COS], tab_ref[TAB_ATT_SIN])
                elif kind == SILU:
                    a = a * jax.nn.sigmoid(a)
                store(g, a)

    @pl.when(i == 0)
    def _():
        for j in range(IN_STAGE_SLOTS):
            stage_copy(j).start()
        normalise(x0_ref, ha_ref)
        step(ha_ref, hb_ref, True)

    @pl.when((i > 0) & (i % 2 == 0))
    def _():
        step(ha_ref, hb_ref, False)

    @pl.when(i % 2 == 1)
    def _():
        step(hb_ref, ha_ref, False)


def _in_proj(x2, norm_w, w_in, tables, seq):
    m = x2.shape[0]
    tiles_per_seq = seq // IN_TM
    assert tables.shape == (N_TABLES, seq, HEAD_DIM) and seq % IN_TM == 0 and m % seq == 0
    tok = lambda width: pl.BlockSpec((IN_TM, width), lambda i: (i, 0))
    tok_shape = lambda width: jax.ShapeDtypeStruct((m, width), BF16)
    return pl.pallas_call(
        _in_proj_kernel,
        grid=(m // IN_TM,),
        in_specs=[
            pl.BlockSpec((IN_TM, D_MODEL), lambda i: (0, 0), pipeline_mode=pl.Buffered(1)),
            pl.BlockSpec((IN_TM, D_MODEL), lambda i: (jnp.minimum(i + 1, m // IN_TM - 1), 0)),
            pl.BlockSpec((1, D_MODEL), lambda i: (0, 0)),
            pl.BlockSpec(memory_space=pl.ANY),
            pl.BlockSpec((N_TABLES, IN_TM, HEAD_DIM), lambda i: (0, i % tiles_per_seq, 0)),
        ],
        out_specs=[pl.BlockSpec((COL_AQ, IN_TM, HEAD_DIM), lambda i: (0, i, 0)),
                   tok(ATT_WIDTH), tok(ATT_KV_WIDTH), tok(ATT_KV_WIDTH), tok(ATT_WIDTH)],
        out_shape=[jax.ShapeDtypeStruct((COL_AQ, m, HEAD_DIM), BF16),
                   tok_shape(ATT_WIDTH), tok_shape(ATT_KV_WIDTH), tok_shape(ATT_KV_WIDTH), tok_shape(ATT_WIDTH)],
        scratch_shapes=[
            pltpu.VMEM((D_MODEL, D_IN_PROJ), BF16),
            pltpu.VMEM((IN_STAGE_SLOTS, D_MODEL, IN_STAGE_COLS), F32),
            pltpu.SemaphoreType.DMA((IN_STAGE_SLOTS,)),
            pltpu.VMEM((IN_TM, D_MODEL), BF16), pltpu.VMEM((IN_TM, D_MODEL), BF16),
        ],
        compiler_params=pltpu.CompilerParams(dimension_semantics=("arbitrary",), vmem_limit_bytes=VMEM_LIMIT),
        name="in_proj",
    )(x2, x2, norm_w.reshape(1, D_MODEL), w_in, tables)


def _log_sigmoid(z):
    return jnp.minimum(z, 0.0) - jnp.log1p(jnp.exp(-jnp.abs(z)))


def _retention_kernel(dec_ref, q_ref, k_ref, v_ref, g_ref, gnw_ref, gnb_ref, o_ref, kv_ref, st_ref, *, seq):
    C = RET_CHUNK
    nc = seq // C
    U = RET_UNROLL
    scale = HEAD_DIM ** -0.5

    lg_f = _log_sigmoid(dec_ref[0, 0:1, :])
    lg_b = _log_sigmoid(dec_ref[0, 1:2, :])
    diff = (lax.broadcasted_iota(jnp.int32, (C, C), 0) - lax.broadcasted_iota(jnp.int32, (C, C), 1)).astype(F32)
    dmat = scale * jnp.where(diff >= 0.0,
                             jnp.exp(lg_f * jnp.maximum(diff, 0.0)),
                             jnp.exp(lg_b * jnp.maximum(-diff, 0.0)))
    lg_f, lg_b = lg_f[:, :HEAD_DIM], lg_b[:, :HEAD_DIM]
    row = lax.broadcasted_iota(jnp.int32, (C, HEAD_DIM), 0).astype(F32)
    kdec_f = (scale * jnp.exp(lg_f * (C - 1.0 - row))).astype(BF16)
    kdec_b = (scale * jnp.exp(lg_b * row)).astype(BF16)
    qdec_f = jnp.exp(lg_f * (row + 1.0)).astype(BF16)
    qdec_b = jnp.exp(lg_b * (C - row)).astype(BF16)
    cdec_f = jnp.exp(lg_f * float(C))
    cdec_b = jnp.exp(lg_b * float(C))

    def chunk_rows(n):
        return pl.ds(pl.multiple_of(n * C, C), C)

    def kv_body(it, carry):
        for u in range(U):
            n = it * U + u
            rows = chunk_rows(n)
            k = k_ref[0, rows, :]
            kd = jnp.concatenate([k * kdec_f, k * kdec_b], axis=1)
            kv_ref[n] = _tn_dot(kd, v_ref[0, rows, :])
        return carry

    lax.fori_loop(0, nc // U, kv_body, 0)

    def scan_body(t, carry):
        sf, sb = carry
        m = nc - 1 - t
        st_ref[t, 0:HEAD_DIM, :] = sf.astype(BF16)
        st_ref[m, HEAD_DIM:2 * HEAD_DIM, :] = sb.astype(BF16)
        sf = sf * cdec_f + kv_ref[t, 0:HEAD_DIM, :]
        sb = sb * cdec_b + kv_ref[m, HEAD_DIM:2 * HEAD_DIM, :]
        return sf, sb

    zero = jnp.zeros((HEAD_DIM, HEAD_DIM), F32)
    lax.fori_loop(0, nc, scan_body, (zero, zero))

    gnw = gnw_ref[...]
    gnb = gnb_ref[...]

    def out_body(it, carry):
        for u in range(U):
            n = it * U + u
            rows = chunk_rows(n)
            q = q_ref[0, rows, :]
            vn = v_ref[0, rows, :]
            scores = _nt_dot(q, k_ref[0, rows, :]) * dmat
            inner = jnp.dot(scores.astype(BF16), vn, preferred_element_type=F32)
            qcat = jnp.concatenate([q * qdec_f, q * qdec_b], axis=1)
            y = inner + jnp.dot(qcat, st_ref[n], preferred_element_type=F32)
            mu = jnp.mean(y, axis=-1, keepdims=True)
            yc = y - mu
            var = jnp.mean(yc * yc, axis=-1, keepdims=True)
            yn = yc * lax.rsqrt(var + GN_EPS) * gnw + gnb
            o_ref[0, rows, :] = (yn * g_ref[0, rows, :].astype(F32)).astype(o_ref.dtype)
        return carry

    lax.fori_loop(0, nc // U, out_body, 0)


def _retention(ret, dec, gn_w, gn_b, batch, seq):
    blk = lambda off: pl.BlockSpec((1, seq, HEAD_DIM), lambda b, h: (off + h, b, 0))
    per_head = pl.BlockSpec((1, HEAD_DIM), lambda b, h: (0, h))
    return pl.pallas_call(
        functools.partial(_retention_kernel, seq=seq),
        grid=(batch, RET_HEADS),
        in_specs=[
            pl.BlockSpec((1, 8, RET_CHUNK), lambda b, h: (h, 0, 0)),
            blk(COL_RQ), blk(COL_RK), blk(COL_RV), blk(COL_RG),
            per_head, per_head,
        ],
        out_specs=pl.BlockSpec((1, seq, HEAD_DIM), lambda b, h: (h, b, 0)),
        out_shape=jax.ShapeDtypeStruct((RET_HEADS, batch * seq, HEAD_DIM), BF16),
        scratch_shapes=[
            pltpu.VMEM((seq // RET_CHUNK, 2 * HEAD_DIM, HEAD_DIM), F32),
            pltpu.VMEM((seq // RET_CHUNK, 2 * HEAD_DIM, HEAD_DIM), BF16),
        ],
        compiler_params=pltpu.CompilerParams(
            dimension_semantics=("arbitrary", "arbitrary"), vmem_limit_bytes=VMEM_LIMIT),
        name="retention",
    )(dec, ret, ret, ret, ret, gn_w.reshape(1, RET_WIDTH), gn_b.reshape(1, RET_WIDTH))


def _attention_kernel(sink_ref, q_ref, k_ref, v_ref, g0_ref, g1_ref, o_ref, vx_ref, *, seq):
    T = WINDOW
    i = pl.program_id(1)

    @pl.when(i == 0)
    def _():
        ones = jnp.ones((T, HEAD_DIM), BF16)

        def body(c, carry):
            rows = pl.ds(pl.multiple_of(c * T, T), T)
            for kh in range(ATT_KV_HEADS):
                vx_ref[rows, 2 * kh * HEAD_DIM:(2 * kh + 1) * HEAD_DIM] = v_ref[rows, kh * HEAD_DIM:(kh + 1) * HEAD_DIM]
                vx_ref[rows, (2 * kh + 1) * HEAD_DIM:(2 * kh + 2) * HEAD_DIM] = ones
            return carry

        lax.fori_loop(0, seq // T, body, 0)

    g_refs = (g0_ref, g1_ref)
    qrow = lax.broadcasted_iota(jnp.int32, (T, 3 * T), 0)
    kcol = lax.broadcasted_iota(jnp.int32, (T, 3 * T), 1)

    for t in range(ATT_TQ // T):
        rows = slice(t * T, (t + 1) * T)
        qstart = (i * (ATT_TQ // T) + t) * T
        kstart = pl.multiple_of(jnp.clip(qstart - T, 0, seq - 3 * T), T)
        bias = jnp.where(jnp.abs(qrow - kcol + (qstart - kstart)) <= WINDOW, 0.0, MASK_VALUE)
        for kh in range(ATT_KV_HEADS):
            q4 = jnp.concatenate(
                [q_ref[rows, (kh * ATT_GROUP + g) * HEAD_DIM:(kh * ATT_GROUP + g + 1) * HEAD_DIM]
                 for g in range(ATT_GROUP)], axis=0)
            kw = k_ref[pl.ds(kstart, 3 * T), kh * HEAD_DIM:(kh + 1) * HEAD_DIM]
            vw = vx_ref[pl.ds(kstart, 3 * T), 2 * kh * HEAD_DIM:(2 * kh + 2) * HEAD_DIM]
            ps, ms, sinks = [], [], []
            for g in range(ATT_GROUP):
                s = _nt_dot(q4[g * T:(g + 1) * T], kw) + bias
                sink = sink_ref[kh * ATT_GROUP + g] * LOG2E
                m = jnp.maximum(jnp.max(s, axis=-1, keepdims=True), sink)
                ps.append(jnp.exp2(s - m).astype(BF16))
                ms.append(m)
                sinks.append(sink)
            pv = jnp.dot(jnp.concatenate(ps, axis=0), vw, preferred_element_type=F32)
            for g in range(ATT_GROUP):
                blk = pv[g * T:(g + 1) * T]
                denom = blk[:, HEAD_DIM:HEAD_DIM + 1] + jnp.exp2(sinks[g] - ms[g])
                gate = g_refs[kh][rows, g * HEAD_DIM:(g + 1) * HEAD_DIM].astype(F32)
                ocols = slice((kh * ATT_GROUP + g) * HEAD_DIM, (kh * ATT_GROUP + g + 1) * HEAD_DIM)
                o_ref[rows, ocols] = (blk[:, :HEAD_DIM] / denom * gate).astype(o_ref.dtype)


def _attention(aq, ak, av, ag, sink, batch, seq):
    nq = seq // ATT_TQ
    gw = ATT_GROUP * HEAD_DIM
    return pl.pallas_call(
        functools.partial(_attention_kernel, seq=seq),
        grid=(batch, nq),
        in_specs=[
            pl.BlockSpec(memory_space=pltpu.SMEM),
            pl.BlockSpec((ATT_TQ, ATT_WIDTH), lambda b, i: (b * nq + i, 0)),
            pl.BlockSpec((seq, ATT_KV_WIDTH), lambda b, i: (b, 0)),
            pl.BlockSpec((seq, ATT_KV_WIDTH), lambda b, i: (b, 0)),
            pl.BlockSpec((ATT_TQ, gw), lambda b, i: (b * nq + i, 0)),
            pl.BlockSpec((ATT_TQ, gw), lambda b, i: (b * nq + i, 1)),
        ],
        out_specs=pl.BlockSpec((ATT_TQ, ATT_WIDTH), lambda b, i: (b * nq + i, 0)),
        out_shape=jax.ShapeDtypeStruct((batch * seq, ATT_WIDTH), BF16),
        scratch_shapes=[pltpu.VMEM((seq, 2 * ATT_KV_WIDTH), BF16)],
        compiler_params=pltpu.CompilerParams(
            dimension_semantics=("arbitrary", "arbitrary"), vmem_limit_bytes=VMEM_LIMIT),
        name="attention",
    )(sink, aq, ak, av, ag, ag)


def _out_proj_kernel(x_ref, yr_ref, ya_ref, w_hbm_ref, fw_ref, o_ref, w_ref, stage_ref, sem_ref):
    n_stage = D_MODEL // OUT_STAGE_COLS
    stage_per_chunk = OUT_CHUNK // OUT_STAGE_COLS

    def stage_copy(j):
        slot = j % OUT_STAGE_SLOTS
        return pltpu.make_async_copy(w_hbm_ref.at[:, pl.ds(j * OUT_STAGE_COLS, OUT_STAGE_COLS)],
                                     stage_ref.at[slot], sem_ref.at[slot])

    def tile(stream_weights):
        for r in range(OUT_TM // OUT_SUB):
            rows = slice(r * OUT_SUB, (r + 1) * OUT_SUB)
            y = jnp.concatenate([yr_ref[h, rows, :] for h in range(RET_HEADS)] + [ya_ref[rows, :]], axis=1)
            ssq = jnp.zeros((OUT_SUB, HEAD_DIM), F32)
            for c in range(D_MODEL // OUT_CHUNK):
                if stream_weights and r == 0:
                    for j in range(c * stage_per_chunk, (c + 1) * stage_per_chunk):
                        stage_copy(j).wait()
                        w_ref[:, j * OUT_STAGE_COLS:(j + 1) * OUT_STAGE_COLS] = stage_ref[j % OUT_STAGE_SLOTS].astype(BF16)
                        if j + OUT_STAGE_SLOTS < n_stage:
                            stage_copy(j + OUT_STAGE_SLOTS).start()
                cols = slice(c * OUT_CHUNK, (c + 1) * OUT_CHUNK)
                z = x_ref[rows, cols] + jnp.dot(y, w_ref[:, cols], preferred_element_type=F32)
                o_ref[rows, cols] = z
                for u in range(OUT_CHUNK // HEAD_DIM):
                    zu = z[:, u * HEAD_DIM:(u + 1) * HEAD_DIM]
                    ssq = ssq + zu * zu
            rs = lax.rsqrt(jnp.sum(ssq, axis=-1, keepdims=True) * (1.0 / D_MODEL) + NORM_EPS)
            for c in range(D_MODEL // OUT_CHUNK):
                cols = slice(c * OUT_CHUNK, (c + 1) * OUT_CHUNK)
                o_ref[rows, cols] = o_ref[rows, cols] * rs * fw_ref[:, cols]

    @pl.when(pl.program_id(0) == 0)
    def _():
        for j in range(OUT_STAGE_SLOTS):
            stage_copy(j).start()
        tile(True)

    @pl.when(pl.program_id(0) > 0)
    def _():
        tile(False)


def _out_proj(x2, y_ret, y_att, w_out, final_norm_w):
    m = x2.shape[0]
    return pl.pallas_call(
        _out_proj_kernel,
        grid=(m // OUT_TM,),
        in_specs=[
            pl.BlockSpec((OUT_TM, D_MODEL), lambda i: (i, 0)),
            pl.BlockSpec((RET_HEADS, OUT_TM, HEAD_DIM), lambda i: (0, i, 0)),
            pl.BlockSpec((OUT_TM, ATT_WIDTH), lambda i: (i, 0)),
            pl.BlockSpec(memory_space=pl.ANY),
            pl.BlockSpec((1, D_MODEL), lambda i: (0, 0)),
        ],
        out_specs=pl.BlockSpec((OUT_TM, D_MODEL), lambda i: (i, 0)),
        out_shape=jax.ShapeDtypeStruct((m, D_MODEL), F32),
        scratch_shapes=[
            pltpu.VMEM((RET_WIDTH + ATT_WIDTH, D_MODEL), BF16),
            pltpu.VMEM((OUT_STAGE_SLOTS, RET_WIDTH + ATT_WIDTH, OUT_STAGE_COLS), F32),
            pltpu.SemaphoreType.DMA((OUT_STAGE_SLOTS,)),
        ],
        compiler_params=pltpu.CompilerParams(
            dimension_semantics=("arbitrary",), vmem_limit_bytes=VMEM_LIMIT),
        name="out_proj",
    )(x2, y_ret, y_att, w_out, final_norm_w.reshape(1, D_MODEL))


def _rotary_tables(seq):
    def cos_sin(theta, half):
        inv_freq = (theta ** (-jnp.arange(half, dtype=F32) / half))[None, :]
        hi = jnp.arange(seq // TRIG_SPLIT, dtype=F32)[:, None] * TRIG_SPLIT * inv_freq
        lo = jnp.arange(TRIG_SPLIT, dtype=F32)[:, None] * inv_freq
        ch, sh, cl, sl = jnp.cos(hi)[:, None], jnp.sin(hi)[:, None], jnp.cos(lo)[None], jnp.sin(lo)[None]
        return (ch * cl - sh * sl).reshape(seq, half), (sh * cl + ch * sl).reshape(seq, half)

    cos_r, sin_r = cos_sin(RET_THETA, HEAD_DIM // 2)
    cos_a, sin_a = cos_sin(ROPE_THETA, ROPE_DIMS // 2)
    rest = jnp.zeros((seq, HEAD_DIM - ROPE_DIMS), F32)
    tabs = [None] * N_TABLES
    tabs[TAB_RET_COS] = jnp.concatenate([cos_r, cos_r], axis=1)
    tabs[TAB_RET_SIN] = jnp.concatenate([-sin_r, sin_r], axis=1)
    tabs[TAB_ATT_COS] = jnp.concatenate([cos_a, cos_a, rest + 1.0], axis=1)
    tabs[TAB_ATT_SIN] = jnp.concatenate([-sin_a, sin_a, rest], axis=1)
    return jnp.stack(tabs)


def _layer(x2, w_in, w_out, norm_w, decay_fwd, decay_bwd, gn_w, gn_b, sink, post_norm_w, batch, seq):
    ret, aq, ak, av, ag = _in_proj(x2, norm_w, w_in.astype(F32), _rotary_tables(seq), seq)
    dec = jnp.zeros((RET_HEADS, 8, RET_CHUNK), F32)
    dec = dec.at[:, 0, :].set(decay_fwd.astype(F32)[:, None])
    dec = dec.at[:, 1, :].set(decay_bwd.astype(F32)[:, None])
    y_ret = _retention(ret, dec, gn_w.astype(F32), gn_b.astype(F32), batch, seq)
    y_att = _attention(aq, ak, av, ag, sink.astype(F32), batch, seq)
    return _out_proj(x2, y_ret, y_att, w_out.astype(F32), post_norm_w)


def kernel(x, w_in, w_out, norm_w, decay_fwd, decay_bwd, gn_w, gn_b, sink, final_norm_w):
    batch, seq, _ = x.shape
    depth = w_in.shape[0]
    assert depth == 1, "the final RMSNorm is fused into the single layer's output projection"
    assert seq % (RET_UNROLL * RET_CHUNK) == 0 and seq % ATT_TQ == 0 and seq >= 3 * WINDOW
    assert (batch * seq) % OUT_TM == 0 and N_COL_GROUPS % IN_CHUNK_GROUPS == 0 and seq % TRIG_SPLIT == 0
    x2 = x.reshape(batch * seq, D_MODEL)
    out = _layer(x2, w_in[0], w_out[0], norm_w[0], decay_fwd[0], decay_bwd[0], gn_w[0], gn_b[0], sink[0],
                 final_norm_w, batch, seq)
    return out.reshape(batch, seq, D_MODEL)
```
